```python
import jax, jax.numpy as jnp
from jax import lax
import numpy as np

D_MODEL = 1024
BATCH = 4
SEQ = 8192
DEPTH = 2

HEAD_DIM = 64
ATTN_GROUPS = ((128, 1), (512, 4), (2048, 16))
N_GROUPS = len(ATTN_GROUPS)
HEADS_PER_GROUP = 6
N_ATTN_HEADS = N_GROUPS * HEADS_PER_GROUP
ATTN_WIDTH = N_ATTN_HEADS * HEAD_DIM
ATTN_OUT_WIDTH = HEADS_PER_GROUP * HEAD_DIM
NUM_BUCKETS = 32
MAX_DISTANCE = 2048
RET_HEADS = 4
RET_QK_DIM = 256
RET_V_DIM = 2 * RET_QK_DIM
RET_QK_WIDTH = RET_HEADS * RET_QK_DIM
RET_V_WIDTH = RET_HEADS * RET_V_DIM
RET_CHUNK = 128
ROPE_BASE = 10000.0
D_FF = -(-8 * D_MODEL // (3 * 256)) * 256
ALPHA = (2 * DEPTH) ** 0.25
BETA = (8 * DEPTH) ** -0.25
LN_EPS = 1e-5
GN_EPS = 1e-5
SPLIT_SIZES = (ATTN_WIDTH, ATTN_WIDTH, ATTN_WIDTH,
               RET_QK_WIDTH, RET_QK_WIDTH, RET_V_WIDTH, RET_V_WIDTH,
               D_MODEL, D_MODEL)
IN_COLS = sum(SPLIT_SIZES)
SPLIT_POINTS = tuple(int(v) for v in np.cumsum(SPLIT_SIZES)[:-1])

kernel_name = "hybrid_dilated_attn_retention_deepnorm"


def _t5_bucket(dist):
    max_exact = NUM_BUCKETS // 2
    large = max_exact + (np.log(np.maximum(dist, max_exact) / max_exact)
                         / np.log(MAX_DISTANCE / max_exact)
                         * (NUM_BUCKETS - max_exact)).astype(np.int32)
    large = np.minimum(large, NUM_BUCKETS - 1)
    return np.where(dist < max_exact, dist, large).astype(np.int32)


def _layer_norm(x, g, b):
    xf = x.astype(jnp.float32)
    mu = jnp.mean(xf, axis=-1, keepdims=True)
    var = jnp.mean(jnp.square(xf - mu), axis=-1, keepdims=True)
    return ((xf - mu) * lax.rsqrt(var + LN_EPS) * g + b).astype(x.dtype)


def _dilated_window_attention(q, k, v, bias_table, window, dilation):
    B, S, H, Dh = q.shape
    W = window // dilation
    L = S // dilation
    nb = -(-L // W)
    Lp = nb * W

    def to_sub(t):
        t = t.reshape(B, L, dilation, H, Dh).transpose(0, 2, 3, 1, 4)
        return jnp.pad(t, ((0, 0), (0, 0), (0, 0), (0, Lp - L), (0, 0)))

    def band(t):
        t = jnp.pad(to_sub(t), ((0, 0), (0, 0), (0, 0), (W, 0), (0, 0)))
        prev = t[:, :, :, :Lp].reshape(B, dilation, H, nb, W, Dh)
        cur = t[:, :, :, W:].reshape(B, dilation, H, nb, W, Dh)
        return jnp.concatenate([prev, cur], axis=-2)

    qs = to_sub(q).reshape(B, dilation, H, nb, W, Dh)
    kb, vb = band(k), band(v)

    qi = np.arange(W)[:, None]
    kj = np.arange(2 * W)[None, :]
    rel = qi + W - kj
    in_win = (rel >= 0) & (rel <= W)
    key_idx = np.arange(nb)[:, None, None] * W + kj[None] - W
    mask = in_win[None] & (key_idx >= 0)
    buckets = _t5_bucket(np.clip(rel, 0, W) * dilation)
    bias = jnp.moveaxis(jnp.take(bias_table, buckets, axis=0), -1, 0).astype(jnp.float32)

    s = jnp.einsum('bghnqe,bghnke->bghnqk', qs, kb).astype(jnp.float32) * (Dh ** -0.5)
    s = s + bias[None, None, :, None]
    s = jnp.where(mask[None, None, None], s, -jnp.inf)
    m = jnp.max(s, axis=-1, keepdims=True)
    p = jnp.exp(s - m)
    l = jnp.sum(p, axis=-1, keepdims=True)
    o = jnp.einsum('bghnqk,bghnke->bghnqe', (p / l).astype(v.dtype), vb)
    lse = (m + jnp.log(l))[..., 0]
    o = o.reshape(B, dilation, H, Lp, Dh)[:, :, :, :L].transpose(0, 3, 1, 2, 4).reshape(B, S, H, Dh)
    lse = lse.reshape(B, dilation, H, Lp)[..., :L].transpose(0, 3, 1, 2).reshape(B, S, H)
    return o, lse


def _retention(q, k, v):
    B, S, H, dk = q.shape
    dv = v.shape[-1]
    half = dk // 2
    pos = jnp.arange(S, dtype=jnp.float32)
    inv_freq = ROPE_BASE ** (-jnp.arange(half, dtype=jnp.float32) / half)
    ang = pos[:, None] * inv_freq[None]
    cos = jnp.cos(ang)[None, :, None]
    sin = jnp.sin(ang)[None, :, None]

    def rot(t):
        t1, t2 = t[..., :half], t[..., half:]
        return jnp.concatenate([t1 * cos - t2 * sin, t1 * sin + t2 * cos], axis=-1).astype(t.dtype)

    q = rot(q)
    k = rot(k) * (dk ** -0.5)
    log_g = jnp.log(1.0 - 2.0 ** (-5.0 - jnp.arange(H, dtype=jnp.float32)))
    C = RET_CHUNK
    nC = S // C
    n = jnp.arange(C, dtype=jnp.float32)
    diff = n[:, None] - n[None, :]
    decay_mask = jnp.where(diff >= 0, jnp.exp(log_g[:, None, None] * jnp.maximum(diff, 0.0)), 0.0)
    q_dec = jnp.exp(log_g[:, None] * (n + 1.0))
    k_dec = jnp.exp(log_g[:, None] * (C - 1.0 - n))
    chunk_dec = jnp.exp(log_g * C)

    def chunks(t):
        return t.reshape(B, nC, C, H, t.shape[-1]).transpose(1, 0, 3, 2, 4)

    def step(state, xs):
        qc, kc, vc = xs
        sc = jnp.einsum('bhnd,bhmd->bhnm', qc, kc) * decay_mask
        o = (jnp.einsum('bhnm,bhmv->bhnv', sc, vc)
             + jnp.einsum('bhnd,bhdv->bhnv', qc * q_dec[..., None], state))
        state = (state * chunk_dec[:, None, None]
                 + jnp.einsum('bhmd,bhmv->bhdv', kc * k_dec[..., None], vc))
        return state.astype(jnp.float32), o.astype(jnp.float32)

    state0 = jnp.zeros((B, H, dk, dv), jnp.float32)
    _, ys = lax.scan(step, state0, (chunks(q), chunks(k), chunks(v)))
    o = ys.transpose(1, 0, 3, 2, 4).reshape(B, S, H, dv)
    mu = jnp.mean(o, axis=-1, keepdims=True)
    var = jnp.mean(jnp.square(o - mu), axis=-1, keepdims=True)
    return (o - mu) * lax.rsqrt(var + GN_EPS)


def _hybrid_mixer(x, rel_bias, w_in, b_in, w_attn_proj, w_ret_proj, w_out):
    B, S, _ = x.shape
    z = jnp.einsum('bsd,dc->bsc', x, w_in) + b_in
    q_a, k_a, v_a, q_r, k_r, v_r, g_r, gate_a, gate_b = jnp.split(z, SPLIT_POINTS, axis=-1)
    q_a = q_a.reshape(B, S, N_GROUPS, HEADS_PER_GROUP, HEAD_DIM)
    k_a = k_a.reshape(B, S, N_GROUPS, HEADS_PER_GROUP, HEAD_DIM)
    v_a = v_a.reshape(B, S, N_GROUPS, HEADS_PER_GROUP, HEAD_DIM)
    outs, lses = [], []
    for gi, (window, dilation) in enumerate(ATTN_GROUPS):
        o, lse = _dilated_window_attention(
            q_a[:, :, gi], k_a[:, :, gi], v_a[:, :, gi],
            rel_bias[:, gi * HEADS_PER_GROUP:(gi + 1) * HEADS_PER_GROUP], window, dilation)
        outs.append(o.astype(jnp.float32))
        lses.append(lse)
    wts = jax.nn.softmax(jnp.stack(lses, axis=0), axis=0)
    y_a = jnp.sum(wts[..., None] * jnp.stack(outs, axis=0), axis=0)
    y_a = y_a.astype(x.dtype).reshape(B, S, ATTN_OUT_WIDTH)
    y_r = _retention(q_r.reshape(B, S, RET_HEADS, RET_QK_DIM),
                     k_r.reshape(B, S, RET_HEADS, RET_QK_DIM),
                     v_r.reshape(B, S, RET_HEADS, RET_V_DIM))
    y_b = (jax.nn.silu(g_r) * y_r.reshape(B, S, RET_V_WIDTH)).astype(x.dtype)
    merged = (jax.nn.sigmoid(gate_a) * jnp.einsum('bsc,cd->bsd', y_a, w_attn_proj)
              + jax.nn.sigmoid(gate_b) * jnp.einsum('bsc,cd->bsd', y_b, w_ret_proj))
    return jnp.einsum('bsd,de->bse', merged, w_out)


def _swiglu(x, w_gate, w_up, w_down):
    h = jax.nn.silu(jnp.einsum('bsd,df->bsf', x, w_gate)) * jnp.einsum('bsd,df->bsf', x, w_up)
    return jnp.einsum('bsf,fd->bsd', h, w_down)


def setup_inputs(seed: int = 0) -> dict:
    key = jax.random.key(seed)
    ks = jax.random.split(key, 14)
    f32 = jnp.float32

    def nrm(k, shape, scale):
        return jax.random.normal(k, shape, f32) * scale

    return {
        "x": nrm(ks[0], (BATCH, SEQ, D_MODEL), 1.0),
        "rel_bias": nrm(ks[1], (NUM_BUCKETS, N_ATTN_HEADS), 0.2),
        "w_in": nrm(ks[2], (DEPTH, D_MODEL, IN_COLS), D_MODEL ** -0.5),
        "b_in": nrm(ks[3], (DEPTH, IN_COLS), 0.02),
        "w_attn_proj": nrm(ks[4], (DEPTH, ATTN_OUT_WIDTH, D_MODEL), BETA * ATTN_OUT_WIDTH ** -0.5),
        "w_ret_proj": nrm(ks[5], (DEPTH, RET_V_WIDTH, D_MODEL), BETA * RET_V_WIDTH ** -0.5),
        "w_out": nrm(ks[6], (DEPTH, D_MODEL, D_MODEL), BETA * D_MODEL ** -0.5),
        "ln1_g": 1.0 + nrm(ks[7], (DEPTH, D_MODEL), 0.02),
        "ln1_b": nrm(ks[8], (DEPTH, D_MODEL), 0.02),
        "w_ffn_gate": nrm(ks[9], (DEPTH, D_MODEL, D_FF), D_MODEL ** -0.5),
        "w_ffn_up": nrm(ks[10], (DEPTH, D_MODEL, D_FF), D_MODEL ** -0.5),
        "w_ffn_down": nrm(ks[11], (DEPTH, D_FF, D_MODEL), BETA * D_FF ** -0.5),
        "ln2_g": 1.0 + nrm(ks[12], (DEPTH, D_MODEL), 0.02),
        "ln2_b": nrm(ks[13], (DEPTH, D_MODEL), 0.02),
    }


def reference(x, rel_bias, w_in, b_in, w_attn_proj, w_ret_proj, w_out, ln1_g, ln1_b,
              w_ffn_gate, w_ffn_up, w_ffn_down, ln2_g, ln2_b):
    for l in range(DEPTH):
        mix = _hybrid_mixer(x, rel_bias, w_in[l], b_in[l], w_attn_proj[l], w_ret_proj[l], w_out[l])
        x = _layer_norm(ALPHA * x + mix, ln1_g[l], ln1_b[l])
        ffn = _swiglu(x, w_ffn_gate[l], w_ffn_up[l], w_ffn_down[l])
        x = _layer_norm(ALPHA * x + ffn, ln2_g[l], ln2_b[l])
    return x
```

```python
import functools

import numpy as np
import jax
import jax.numpy as jnp
from jax import lax
from jax.experimental import pallas as pl
from jax.experimental.pallas import tpu as pltpu

F32 = jnp.float32
BF16 = jnp.bfloat16

D_MODEL = 1024
DEPTH = 2
HEAD_DIM = 64
ATTN_GROUPS = ((128, 1), (512, 4), (2048, 16))
N_GROUPS = len(ATTN_GROUPS)
HEADS_PER_GROUP = 6
GROUP_WIDTH = HEADS_PER_GROUP * HEAD_DIM
ATTN_WIDTH = N_GROUPS * GROUP_WIDTH
NUM_BUCKETS = 32
MAX_DISTANCE = 2048
ATTN_BLOCK = 128
RET_HEADS = 4
RET_QK_DIM = 256
RET_V_DIM = 512
RET_QK_WIDTH = RET_HEADS * RET_QK_DIM
RET_V_WIDTH = RET_HEADS * RET_V_DIM
RET_CHUNK = 128
ROPE_BASE = 10000.0
D_FF = 2816
ALPHA = (2 * DEPTH) ** 0.25
LN_EPS = 1e-5
GN_EPS = 1e-5
MASK_VALUE = -1e30

_COL_ATTN = (0, 3 * ATTN_WIDTH)
_COL_RQK = (_COL_ATTN[1], _COL_ATTN[1] + 2 * RET_QK_WIDTH)
_COL_REST = (_COL_RQK[1], _COL_RQK[1] + 2 * RET_V_WIDTH + 2 * D_MODEL)

VMEM_LIMIT = 56 * 1024 * 1024


def _params(*sem):
    return pltpu.CompilerParams(dimension_semantics=sem, vmem_limit_bytes=VMEM_LIMIT)


def _proj_attn_kernel(x_ref, w_ref, b_ref, o_ref):
    j = pl.program_id(1)
    acc = jnp.dot(x_ref[...].astype(BF16), w_ref[...], preferred_element_type=F32) + b_ref[...]
    scale = jnp.where(j == 0, HEAD_DIM ** -0.5, 1.0).astype(F32)
    o_ref[...] = (acc * scale).astype(o_ref.dtype)


def _proj_rot_kernel(x_ref, w_ref, b_ref, cos_ref, sin_ref, o_ref):
    j = pl.program_id(1)
    acc = jnp.dot(x_ref[...].astype(BF16), w_ref[...], preferred_element_type=F32) + b_ref[...]
    scale = jnp.where(j == 0, 1.0, RET_QK_DIM ** -0.5).astype(F32)
    c = cos_ref[...] * scale
    s = sin_ref[...] * scale
    half = RET_QK_DIM // 2
    for h in range(RET_HEADS):
        lo = h * RET_QK_DIM
        t1 = acc[:, lo:lo + half]
        t2 = acc[:, lo + half:lo + RET_QK_DIM]
        o_ref[:, lo:lo + half] = (t1 * c - t2 * s).astype(o_ref.dtype)
        o_ref[:, lo + half:lo + RET_QK_DIM] = (t1 * s + t2 * c).astype(o_ref.dtype)


def _proj_rest_kernel(x_ref, w_ref, b_ref, o_ref, *, tiles_per_section):
    j = pl.program_id(1)
    acc = jnp.dot(x_ref[...].astype(BF16), w_ref[...], preferred_element_type=F32) + b_ref[...]

    @pl.when(j < tiles_per_section)
    def _():
        o_ref[...] = acc.astype(o_ref.dtype)

    @pl.when((j >= tiles_per_section) & (j < 2 * tiles_per_section))
    def _():
        o_ref[...] = (acc * jax.nn.sigmoid(acc)).astype(o_ref.dtype)

    @pl.when(j >= 2 * tiles_per_section)
    def _():
        o_ref[...] = jax.nn.sigmoid(acc).astype(o_ref.dtype)


def _project(kernel, x, w, b, tn, extra=(), extra_specs=(), tm=1024):
    n, d = x.shape
    cols = w.shape[1]
    return pl.pallas_call(
        kernel,
        grid=(n // tm, cols // tn),
        in_specs=[pl.BlockSpec((tm, d), lambda i, j: (i, 0)),
                  pl.BlockSpec((d, tn), lambda i, j: (0, j)),
                  pl.BlockSpec((1, tn), lambda i, j: (0, j)),
                  *extra_specs],
        out_specs=pl.BlockSpec((tm, tn), lambda i, j: (i, j)),
        out_shape=jax.ShapeDtypeStruct((n, cols), BF16),
        compiler_params=_params("parallel", "arbitrary"),
    )(x, w, b.reshape(1, cols), *extra)


def _attn_kernel(q_ref, kp_ref, kc_ref, vp_ref, vc_ref, bias_ref, o_ref, lse_ref, *, sub_blocks):
    first_step = pl.program_id(2) == 0
    w = ATTN_BLOCK
    lane = lax.broadcasted_iota(jnp.int32, (w, 128), 1)
    for j in range(sub_blocks):
        rows = slice(j * w, (j + 1) * w)
        lse_tile = jnp.zeros((w, 128), F32)
        for h in range(HEADS_PER_GROUP):
            cols = slice(h * HEAD_DIM, (h + 1) * HEAD_DIM)
            q = q_ref[0, rows, cols]
            if j == 0:
                k = jnp.concatenate([kp_ref[0, :, cols], kc_ref[0, :w, cols]], axis=0)
                v = jnp.concatenate([vp_ref[0, :, cols], vc_ref[0, :w, cols]], axis=0)
                bias = bias_ref[jnp.where(first_step, 0, 1), h]
            else:
                k = kc_ref[0, (j - 1) * w:(j + 1) * w, cols]
                v = vc_ref[0, (j - 1) * w:(j + 1) * w, cols]
                bias = bias_ref[1, h]
            s = lax.dot_general(q, k, (((1,), (1,)), ((), ())), preferred_element_type=F32) + bias
            m = jnp.max(s, axis=-1, keepdims=True)
            p = jnp.exp(s - m)
            l = jnp.sum(p, axis=-1, keepdims=True)
            o = jnp.dot(p.astype(BF16), v, preferred_element_type=F32) / l
            o_ref[0, rows, cols] = o.astype(o_ref.dtype)
            lse_tile = jnp.where(lane == h, m + jnp.log(l), lse_tile)
        lse_ref[0, rows, :] = lse_tile


def _attention_group(qkv, bias, gi, dilation, batch, seq, q_rows):
    sub_len = seq // dilation
    q_rows = min(q_rows, sub_len)
    sub_blocks = q_rows // ATTN_BLOCK
    steps = sub_len // q_rows
    units = 3 * ATTN_WIDTH // GROUP_WIDTH
    view = qkv.reshape(batch, sub_len, dilation * 3 * ATTN_WIDTH)

    def cur(unit):
        return pl.BlockSpec((1, q_rows, GROUP_WIDTH), lambda b, r, n: (b, n, r * units + unit))

    def prev(unit):
        return pl.BlockSpec((1, ATTN_BLOCK, GROUP_WIDTH),
                            lambda b, r, n: (b, jnp.maximum(n * sub_blocks - 1, 0), r * units + unit))

    o, lse = pl.pallas_call(
        functools.partial(_attn_kernel, sub_blocks=sub_blocks),
        grid=(batch, dilation, steps),
        in_specs=[cur(gi), prev(N_GROUPS + gi), cur(N_GROUPS + gi), prev(2 * N_GROUPS + gi),
                  cur(2 * N_GROUPS + gi),
                  pl.BlockSpec((2, HEADS_PER_GROUP, ATTN_BLOCK, 2 * ATTN_BLOCK), lambda b, r, n: (0, 0, 0, 0))],
        out_specs=[pl.BlockSpec((1, q_rows, GROUP_WIDTH), lambda b, r, n: (b, n, r)),
                   pl.BlockSpec((1, q_rows, 128), lambda b, r, n: (b, n, r))],
        out_shape=[jax.ShapeDtypeStruct((batch, sub_len, dilation * GROUP_WIDTH), BF16),
                   jax.ShapeDtypeStruct((batch, sub_len, dilation * 128), F32)],
        compiler_params=_params("parallel", "parallel", "arbitrary"),
    )(view, view, view, view, view, bias)
    return o.reshape(batch * seq, GROUP_WIDTH), lse.reshape(batch * seq, 128)


def _t5_bucket(dist):
    max_exact = NUM_BUCKETS // 2
    large = max_exact + (np.log(np.maximum(dist, max_exact) / max_exact)
                         / np.log(MAX_DISTANCE / max_exact)
                         * (NUM_BUCKETS - max_exact)).astype(np.int32)
    large = np.minimum(large, NUM_BUCKETS - 1)
    return np.where(dist < max_exact, dist, large).astype(np.int32)


def _attention_bias(rel_bias_group, dilation):
    w = ATTN_BLOCK
    qi = np.arange(w)[:, None]
    kj = np.arange(2 * w)[None, :]
    rel = qi + w - kj
    in_win = (rel >= 0) & (rel <= w)
    buckets = _t5_bucket(np.clip(rel, 0, w) * dilation)
    bias = jnp.moveaxis(jnp.take(rel_bias_group, buckets, axis=0), -1, 0).astype(F32)
    normal = jnp.where(in_win[None], bias, MASK_VALUE)
    first = jnp.where((in_win & (kj >= w))[None], bias, MASK_VALUE)
    return jnp.stack([first, normal], axis=0)


def _retention_kernel(q_ref, k_ref, v_ref, g_ref, mask_ref, qdec_ref, kdec_ref, cdec_ref, o_ref, state_ref):
    @pl.when(pl.program_id(2) == 0)
    def _():
        state_ref[...] = jnp.zeros_like(state_ref)

    q = q_ref[0]
    k = k_ref[0]
    v = v_ref[0]
    state = state_ref[...]
    sc = lax.dot_general(q, k, (((1,), (1,)), ((), ())), preferred_element_type=F32) * mask_ref[0]
    inner = jnp.dot(sc.astype(BF16), v, preferred_element_type=F32)
    cross = jnp.dot(q, state.astype(BF16), preferred_element_type=F32) * qdec_ref[0]
    o = inner + cross
    kd = (k.astype(F32) * kdec_ref[0]).astype(BF16)
    state_ref[...] = state * cdec_ref[0] + lax.dot_general(
        kd, v, (((0,), (0,)), ((), ())), preferred_element_type=F32)
    mu = jnp.mean(o, axis=-1, keepdims=True)
    cen = o - mu
    var = jnp.mean(cen * cen, axis=-1, keepdims=True)
    y = cen * lax.rsqrt(var + GN_EPS)
    o_ref[0] = (g_ref[0].astype(F32) * y).astype(o_ref.dtype)


def _retention_constants(chunk):
    log_g = np.log(1.0 - 2.0 ** (-5.0 - np.arange(RET_HEADS, dtype=np.float64)))
    n = np.arange(chunk, dtype=np.float64)
    diff = n[:, None] - n[None, :]
    mask = np.where(diff >= 0, np.exp(log_g[:, None, None] * np.maximum(diff, 0.0)), 0.0)
    q_dec = np.exp(log_g[:, None] * (n + 1.0))
    k_dec = np.exp(log_g[:, None] * (chunk - 1.0 - n))
    c_dec = np.exp(log_g * chunk)
    return (jnp.asarray(mask, F32),
            jnp.asarray(np.broadcast_to(q_dec[:, :, None], (RET_HEADS, chunk, RET_V_DIM)), F32),
            jnp.asarray(np.broadcast_to(k_dec[:, :, None], (RET_HEADS, chunk, RET_QK_DIM)), F32),
            jnp.asarray(np.broadcast_to(c_dec[:, None, None], (RET_HEADS, 1, RET_V_DIM)), F32))


def _retention(rqk, rest, batch, seq, chunk):
    mask, q_dec, k_dec, c_dec = _retention_constants(chunk)
    rqk3 = rqk.reshape(batch, seq, 2 * RET_QK_WIDTH)
    rest3 = rest.reshape(batch, seq, rest.shape[1])
    out = pl.pallas_call(
        _retention_kernel,
        grid=(batch, RET_HEADS, seq // chunk),
        in_specs=[pl.BlockSpec((1, chunk, RET_QK_DIM), lambda b, h, c: (b, c, h)),
                  pl.BlockSpec((1, chunk, RET_QK_DIM), lambda b, h, c: (b, c, RET_HEADS + h)),
                  pl.BlockSpec((1, chunk, RET_V_DIM), lambda b, h, c: (b, c, h)),
                  pl.BlockSpec((1, chunk, RET_V_DIM), lambda b, h, c: (b, c, RET_HEADS + h)),
                  pl.BlockSpec((1, chunk, chunk), lambda b, h, c: (h, 0, 0)),
                  pl.BlockSpec((1, chunk, RET_V_DIM), lambda b, h, c: (h, 0, 0)),
                  pl.BlockSpec((1, chunk, RET_QK_DIM), lambda b, h, c: (h, 0, 0)),
                  pl.BlockSpec((1, 1, RET_V_DIM), lambda b, h, c: (h, 0, 0))],
        out_specs=pl.BlockSpec((1, chunk, RET_V_DIM), lambda b, h, c: (b, c, h)),
        out_shape=jax.ShapeDtypeStruct((batch, seq, RET_V_WIDTH), BF16),
        scratch_shapes=[pltpu.VMEM((RET_QK_DIM, RET_V_DIM), F32)],
        compiler_params=_params("parallel", "parallel", "arbitrary"),
    )(rqk3, rqk3, rest3, rest3, mask, q_dec, k_dec, c_dec)
    return out.reshape(batch * seq, RET_V_WIDTH)


def _layer_norm(y, g, b):
    mu = jnp.mean(y, axis=-1, keepdims=True)
    cen = y - mu
    var = jnp.mean(cen * cen, axis=-1, keepdims=True)
    return cen * lax.rsqrt(var + LN_EPS) * g + b


def _merge_kernel(o0_ref, o1_ref, o2_ref, l0_ref, l1_ref, l2_ref, yb_ref, gates_ref, x_ref,
                  expand_ref, wa_ref, wr_ref, wo_ref, g_ref, b_ref, out_ref):
    lses = (l0_ref[...], l1_ref[...], l2_ref[...])
    m = jnp.maximum(jnp.maximum(lses[0], lses[1]), lses[2])
    es = [jnp.exp(l - m) for l in lses]
    inv = 1.0 / (es[0] + es[1] + es[2])
    expand = expand_ref[...]
    y_a = jnp.zeros(o0_ref.shape, F32)
    for e, o_ref in zip(es, (o0_ref, o1_ref, o2_ref)):
        wgt = e * inv
        hi = wgt.astype(BF16)
        lo = (wgt - hi.astype(F32)).astype(BF16)
        wide = (jnp.dot(hi, expand, preferred_element_type=F32)
                + jnp.dot(lo, expand, preferred_element_type=F32))
        y_a = y_a + wide * o_ref[...].astype(F32)
    gates = gates_ref[...].astype(F32)
    branch_a = jnp.dot(y_a.astype(BF16), wa_ref[...], preferred_element_type=F32)
    branch_b = jnp.dot(yb_ref[...], wr_ref[...], preferred_element_type=F32)
    merged = gates[:, :D_MODEL] * branch_a + gates[:, D_MODEL:] * branch_b
    mix = jnp.dot(merged.astype(BF16), wo_ref[...], preferred_element_type=F32)
    out_ref[...] = _layer_norm(ALPHA * x_ref[...] + mix, g_ref[...], b_ref[...])


def _merge(os_, lses, y_b, rest, x, wa, wr, wo, g, b, tm=512):
    n = x.shape[0]
    expand = jnp.asarray(
        (np.arange(128)[:, None] == (np.arange(GROUP_WIDTH)[None, :] // HEAD_DIM)).astype(np.float32), BF16)
    gate_block = (2 * RET_V_WIDTH) // (2 * D_MODEL)
    row = lambda width: pl.BlockSpec((tm, width), lambda i: (i, 0))
    full = lambda a: pl.BlockSpec(a.shape, lambda i: (0,) * a.ndim)
    return pl.pallas_call(
        _merge_kernel,
        grid=(n // tm,),
        in_specs=[row(GROUP_WIDTH)] * 3 + [row(128)] * 3
                 + [row(RET_V_WIDTH), pl.BlockSpec((tm, 2 * D_MODEL), lambda i: (i, gate_block)), row(D_MODEL),
                    full(expand), full(wa), full(wr), full(wo), full(g), full(b)],
        out_specs=row(D_MODEL),
        out_shape=jax.ShapeDtypeStruct((n, D_MODEL), F32),
        compiler_params=_params("parallel"),
    )(*os_, *lses, y_b, rest, x, expand, wa, wr, wo, g, b)


def _ffn_kernel(x_ref, wg_ref, wu_ref, wd_ref, g_ref, b_ref, out_ref):
    x = x_ref[...]
    xb = x.astype(BF16)
    gate = jnp.dot(xb, wg_ref[...], preferred_element_type=F32)
    up = jnp.dot(xb, wu_ref[...], preferred_element_type=F32)
    hidden = (gate * jax.nn.sigmoid(gate) * up).astype(BF16)
    ffn = jnp.dot(hidden, wd_ref[...], preferred_element_type=F32)
    out_ref[...] = _layer_norm(ALPHA * x + ffn, g_ref[...], b_ref[...])


def _ffn(x, wg, wu, wd, g, b, tm=256):
    n = x.shape[0]
    row = pl.BlockSpec((tm, D_MODEL), lambda i: (i, 0))
    full = lambda a: pl.BlockSpec(a.shape, lambda i: (0,) * a.ndim)
    return pl.pallas_call(
        _ffn_kernel,
        grid=(n // tm,),
        in_specs=[row, full(wg), full(wu), full(wd), full(g), full(b)],
        out_specs=row,
        out_shape=jax.ShapeDtypeStruct((n, D_MODEL), F32),
        compiler_params=_params("parallel"),
    )(x, wg, wu, wd, g, b)


def _rope_tables(seq):
    half = RET_QK_DIM // 2
    inv_freq = ROPE_BASE ** (-np.arange(half, dtype=np.float64) / half)
    ang = np.arange(seq, dtype=np.float64)[:, None] * inv_freq[None]
    return jnp.asarray(np.cos(ang), F32), jnp.asarray(np.sin(ang), F32)


def kernel(x, rel_bias, w_in, b_in, w_attn_proj, w_ret_proj, w_out, ln1_g, ln1_b,
           w_ffn_gate, w_ffn_up, w_ffn_down, ln2_g, ln2_b):
    batch, seq, d = x.shape
    n = batch * seq
    assert d == D_MODEL and seq % 2048 == 0
    cos, sin = _rope_tables(seq)
    biases = [_attention_bias(rel_bias[:, gi * HEADS_PER_GROUP:(gi + 1) * HEADS_PER_GROUP], dil)
              for gi, (_, dil) in enumerate(ATTN_GROUPS)]
    xf = x.reshape(n, d)
    tm = 1024
    pos_blocks = seq // tm
    for l in range(DEPTH):
        w = w_in[l].astype(BF16)
        bias_in = b_in[l]
        qkv = _project(_proj_attn_kernel, xf, w[:, _COL_ATTN[0]:_COL_ATTN[1]],
                       bias_in[_COL_ATTN[0]:_COL_ATTN[1]], tn=ATTN_WIDTH, tm=tm)
        rot_spec = pl.BlockSpec((tm, RET_QK_DIM // 2), lambda i, j: (i % pos_blocks, 0))
        rqk = _project(_proj_rot_kernel, xf, w[:, _COL_RQK[0]:_COL_RQK[1]],
                       bias_in[_COL_RQK[0]:_COL_RQK[1]], tn=RET_QK_WIDTH,
                       extra=(cos, sin), extra_specs=(rot_spec, rot_spec), tm=tm)
        tn_rest = 1024
        rest = _project(functools.partial(_proj_rest_kernel, tiles_per_section=RET_V_WIDTH // tn_rest),
                        xf, w[:, _COL_REST[0]:_COL_REST[1]], bias_in[_COL_REST[0]:_COL_REST[1]],
                        tn=tn_rest, tm=tm)
        os_, lses = [], []
        for gi, (_, dil) in enumerate(ATTN_GROUPS):
            o, lse = _attention_group(qkv, biases[gi], gi, dil, batch, seq, q_rows=512)
            os_.append(o)
            lses.append(lse)
        y_b = _retention(rqk, rest, batch, seq, RET_CHUNK)
        xf = _merge(os_, lses, y_b, rest, xf,
                    w_attn_proj[l].astype(BF16), w_ret_proj[l].astype(BF16), w_out[l].astype(BF16),
                    ln1_g[l].reshape(1, d), ln1_b[l].reshape(1, d))
        xf = _ffn(xf, w_ffn_gate[l].astype(BF16), w_ffn_up[l].astype(BF16), w_ffn_down[l].astype(BF16),
                  ln2_g[l].reshape(1, d), ln2_b[l].reshape(1, d))
    return xf.reshape(batch, seq, d)
```

```python
import functools

import numpy as np
import jax
import jax.numpy as jnp
from jax import lax
from jax.experimental import pallas as pl
from jax.experimental.pallas import tpu as pltpu

F32 = jnp.float32
BF16 = jnp.bfloat16

D_MODEL = 1024
DEPTH = 2
HEAD_DIM = 64
ATTN_GROUPS = ((128, 1), (512, 4), (2048, 16))
N_GROUPS = len(ATTN_GROUPS)
HEADS_PER_GROUP = 6
GROUP_WIDTH = HEADS_PER_GROUP * HEAD_DIM
QKV_WIDTH = 3 * GROUP_WIDTH
NUM_BUCKETS = 32
MAX_DISTANCE = 2048
ATTN_BLOCK = 128
LANES = 128
RET_HEADS = 4
RET_QK_DIM = 256
RET_V_DIM = 512
RET_QK_WIDTH = RET_HEADS * RET_QK_DIM
RET_V_WIDTH = RET_HEADS * RET_V_DIM
RET_CHUNK = 128
ROPE_BASE = 10000.0
D_FF = 2816
ALPHA = (2 * DEPTH) ** 0.25
LN_EPS = 1e-5
GN_EPS = 1e-5
MASK_VALUE = -1e30

_ATTN_COLS = 3 * N_GROUPS * GROUP_WIDTH
_COL_RQK = (_ATTN_COLS, _ATTN_COLS + 2 * RET_QK_WIDTH)
_COL_REST = (_COL_RQK[1], _COL_RQK[1] + 2 * RET_V_WIDTH + 2 * D_MODEL)

VMEM_LIMIT = 56 * 1024 * 1024


def _params(*sem):
    return pltpu.CompilerParams(dimension_semantics=sem, vmem_limit_bytes=VMEM_LIMIT)


def _full(a):
    return pl.BlockSpec(a.shape, lambda *_: (0,) * a.ndim)


def _proj_attn_kernel(x_ref, w_ref, b_ref, o0_ref, o1_ref, o2_ref, stage_ref):
    tm = x_ref.shape[0]
    xb = x_ref[...].astype(BF16)
    for gi, ((_, dil), o_ref) in enumerate(zip(ATTN_GROUPS, (o0_ref, o1_ref, o2_ref))):
        sub = tm // dil
        cols = slice(gi * QKV_WIDTH, (gi + 1) * QKV_WIDTH)
        acc = jnp.dot(xb, w_ref[:, cols], preferred_element_type=F32) + b_ref[:, cols]
        q = acc[:, :GROUP_WIDTH] * HEAD_DIM ** -0.5
        if dil == 1:
            o_ref[0, 0, :, :GROUP_WIDTH] = q.astype(o_ref.dtype)
            o_ref[0, 0, :, GROUP_WIDTH:] = acc[:, GROUP_WIDTH:].astype(o_ref.dtype)
            continue
        for c in range(QKV_WIDTH // LANES):
            lanes = slice(c * LANES, (c + 1) * LANES)
            stage_ref[c] = q[:, lanes] if c < GROUP_WIDTH // LANES else acc[:, lanes]
        for r in range(dil):
            for c in range(QKV_WIDTH // LANES):
                o_ref[0, r, :, c * LANES:(c + 1) * LANES] = (
                    stage_ref[c, pl.ds(r, sub, stride=dil), :].astype(o_ref.dtype))


def _project_attn(x, w, b, batch, seq, tm=512):
    n, d = x.shape
    tiles = seq // tm
    outs, specs = [], []
    for _, dil in ATTN_GROUPS:
        outs.append(jax.ShapeDtypeStruct((batch, dil, seq // dil, QKV_WIDTH), BF16))
        specs.append(pl.BlockSpec((1, dil, tm // dil, QKV_WIDTH), lambda i: (i // tiles, 0, i % tiles, 0)))
    return pl.pallas_call(
        _proj_attn_kernel,
        grid=(n // tm,),
        in_specs=[pl.BlockSpec((tm, d), lambda i: (i, 0)), _full(w), _full(b)],
        out_specs=specs,
        out_shape=outs,
        scratch_shapes=[pltpu.VMEM((QKV_WIDTH // LANES, tm, LANES), F32)],
        compiler_params=_params("parallel"),
        name="proj_attn",
    )(x, w, b)


def _proj_rot_kernel(x_ref, w_ref, b_ref, cos_ref, sin_ref, o_ref):
    j = pl.program_id(1)
    acc = jnp.dot(x_ref[...].astype(BF16), w_ref[...], preferred_element_type=F32) + b_ref[...]
    scale = jnp.where(j == 0, 1.0, RET_QK_DIM ** -0.5).astype(F32)
    c = cos_ref[...] * scale
    s = sin_ref[...] * scale
    half = RET_QK_DIM // 2
    for h in range(RET_HEADS):
        lo = h * RET_QK_DIM
        t1 = acc[:, lo:lo + half]
        t2 = acc[:, lo + half:lo + RET_QK_DIM]
        o_ref[:, lo:lo + half] = (t1 * c - t2 * s).astype(o_ref.dtype)
        o_ref[:, lo + half:lo + RET_QK_DIM] = (t1 * s + t2 * c).astype(o_ref.dtype)


def _proj_rest_kernel(x_ref, w_ref, b_ref, o_ref, *, tiles_per_section):
    j = pl.program_id(1)
    acc = jnp.dot(x_ref[...].astype(BF16), w_ref[...], preferred_element_type=F32) + b_ref[...]

    @pl.when(j < tiles_per_section)
    def _():
        o_ref[...] = acc.astype(o_ref.dtype)

    @pl.when((j >= tiles_per_section) & (j < 2 * tiles_per_section))
    def _():
        o_ref[...] = (acc * jax.nn.sigmoid(acc)).astype(o_ref.dtype)

    @pl.when(j >= 2 * tiles_per_section)
    def _():
        o_ref[...] = jax.nn.sigmoid(acc).astype(o_ref.dtype)


def _project(kernel, name, x, w, b, tn, extra=(), extra_specs=(), tm=1024):
    n, d = x.shape
    cols = w.shape[1]
    return pl.pallas_call(
        kernel,
        grid=(n // tm, cols // tn),
        in_specs=[pl.BlockSpec((tm, d), lambda i, j: (i, 0)),
                  pl.BlockSpec((d, tn), lambda i, j: (0, j)),
                  pl.BlockSpec((1, tn), lambda i, j: (0, j)),
                  *extra_specs],
        out_specs=pl.BlockSpec((tm, tn), lambda i, j: (i, j)),
        out_shape=jax.ShapeDtypeStruct((n, cols), BF16),
        compiler_params=_params("parallel", "arbitrary"),
        name=name,
    )(x, w, b.reshape(1, cols), *extra)


def _attn_kernel(q_ref, kp_ref, kc_ref, vp_ref, vc_ref, bias_ref, o_ref, lse_ref, *, sub_blocks):
    first_step = pl.program_id(2) == 0
    w = ATTN_BLOCK
    lane = lax.broadcasted_iota(jnp.int32, (w, LANES), 1)
    low_half = lane < HEAD_DIM
    ones = jnp.ones((2 * w, LANES), BF16)
    for j in range(sub_blocks):
        rows = slice(j * w, (j + 1) * w)
        lse_tile = jnp.zeros((w, LANES), F32)
        for pair in range(HEADS_PER_GROUP // 2):
            cols = slice(pair * LANES, (pair + 1) * LANES)
            q = q_ref[0, 0, rows, cols]
            if j == 0:
                k = jnp.concatenate([kp_ref[0, 0, :, cols], kc_ref[0, 0, :w, cols]], axis=0)
                v = jnp.concatenate([vp_ref[0, 0, :, cols], vc_ref[0, 0, :w, cols]], axis=0)
            else:
                k = kc_ref[0, 0, (j - 1) * w:(j + 1) * w, cols]
                v = vc_ref[0, 0, (j - 1) * w:(j + 1) * w, cols]
            v_ones = jnp.concatenate([v, ones], axis=1)
            halves = []
            for sel, hh in ((low_half, 0), (~low_half, 1)):
                h = 2 * pair + hh
                bias = bias_ref[jnp.where(first_step, 0, 1), h] if j == 0 else bias_ref[1, h]
                qh = jnp.where(sel, q, jnp.zeros_like(q))
                s = lax.dot_general(qh, k, (((1,), (1,)), ((), ())), preferred_element_type=F32) + bias
                m = jnp.max(s, axis=-1, keepdims=True)
                p = jnp.exp(s - m).astype(BF16)
                ov = jnp.dot(p, v_ones, preferred_element_type=F32)
                denom = ov[:, LANES:]
                halves.append(ov[:, :LANES] / denom)
                lse_tile = jnp.where(lane == h, m + jnp.log(denom), lse_tile)
            o_ref[0, 0, rows, cols] = jnp.where(low_half, halves[0], halves[1]).astype(o_ref.dtype)
        lse_ref[0, 0, rows, :] = lse_tile


def _attention_group(qkv, bias, q_rows):
    batch, dilation, sub_len, _ = qkv.shape
    q_rows = min(q_rows, sub_len)
    sub_blocks = q_rows // ATTN_BLOCK
    steps = sub_len // q_rows

    def cur(unit):
        return pl.BlockSpec((1, 1, q_rows, GROUP_WIDTH), lambda b, r, n: (b, r, n, unit))

    def prev(unit):
        return pl.BlockSpec((1, 1, ATTN_BLOCK, GROUP_WIDTH),
                            lambda b, r, n: (b, r, jnp.maximum(n * sub_blocks - 1, 0), unit))

    return pl.pallas_call(
        functools.partial(_attn_kernel, sub_blocks=sub_blocks),
        grid=(batch, dilation, steps),
        in_specs=[cur(0), prev(1), cur(1), prev(2), cur(2), _full(bias)],
        out_specs=[pl.BlockSpec((1, 1, q_rows, GROUP_WIDTH), lambda b, r, n: (b, r, n, 0)),
                   pl.BlockSpec((1, 1, q_rows, LANES), lambda b, r, n: (b, r, n, 0))],
        out_shape=[jax.ShapeDtypeStruct((batch, dilation, sub_len, GROUP_WIDTH), BF16),
                   jax.ShapeDtypeStruct((batch, dilation, sub_len, LANES), F32)],
        compiler_params=_params("parallel", "parallel", "arbitrary"),
        name=f"attn_d{dilation}",
    )(qkv, qkv, qkv, qkv, qkv, bias)


def _t5_bucket(dist):
    max_exact = NUM_BUCKETS // 2
    large = max_exact + (np.log(np.maximum(dist, max_exact) / max_exact)
                         / np.log(MAX_DISTANCE / max_exact)
                         * (NUM_BUCKETS - max_exact)).astype(np.int32)
    large = np.minimum(large, NUM_BUCKETS - 1)
    return np.where(dist < max_exact, dist, large).astype(np.int32)


def _attention_bias(rel_bias_group, dilation):
    w = ATTN_BLOCK
    buckets = _t5_bucket((w - np.arange(w + 1)) * dilation)
    onehot = jnp.asarray(np.eye(NUM_BUCKETS, dtype=np.float32)[buckets])
    per_c = jnp.einsum('cb,bh->hc', onehot, rel_bias_group.astype(F32),
                       precision=lax.Precision.HIGHEST)
    period = 2 * w + 1
    vec = jnp.concatenate([per_c, jnp.full((HEADS_PER_GROUP, period - (w + 1)), MASK_VALUE, F32)], axis=1)
    normal = jnp.tile(vec, (1, w))[:, :w * 2 * w].reshape(HEADS_PER_GROUP, w, 2 * w)
    first = jnp.where(np.arange(2 * w)[None, None, :] >= w, normal, MASK_VALUE)
    return jnp.stack([first, normal], axis=0)


def _retention_kernel(q_ref, k_ref, v_ref, g_ref, mask_ref, qdec_ref, kdec_ref, cdec_ref, o_ref, state_ref):
    @pl.when(pl.program_id(1) == 0)
    def _():
        state_ref[...] = jnp.zeros_like(state_ref)

    for h in range(RET_HEADS):
        qk_cols = slice(h * RET_QK_DIM, (h + 1) * RET_QK_DIM)
        v_cols = slice(h * RET_V_DIM, (h + 1) * RET_V_DIM)
        q = q_ref[0, :, qk_cols]
        k = k_ref[0, :, qk_cols]
        v = v_ref[0, :, v_cols]
        state = state_ref[h]
        sc = lax.dot_general(q, k, (((1,), (1,)), ((), ())), preferred_element_type=F32) * mask_ref[h]
        inner = jnp.dot(sc.astype(BF16), v, preferred_element_type=F32)
        cross = jnp.dot(q, state.astype(BF16), preferred_element_type=F32) * qdec_ref[h]
        o = inner + cross
        kd = (k.astype(F32) * kdec_ref[h]).astype(BF16)
        state_ref[h] = state * cdec_ref[h] + lax.dot_general(
            kd, v, (((0,), (0,)), ((), ())), preferred_element_type=F32)
        mu = jnp.mean(o, axis=-1, keepdims=True)
        cen = o - mu
        var = jnp.mean(cen * cen, axis=-1, keepdims=True)
        y = cen * lax.rsqrt(var + GN_EPS)
        o_ref[0, :, v_cols] = (g_ref[0, :, v_cols].astype(F32) * y).astype(o_ref.dtype)


def _retention_constants(chunk):
    log_g = np.log(1.0 - 2.0 ** (-5.0 - np.arange(RET_HEADS, dtype=np.float64)))
    n = np.arange(chunk, dtype=np.float64)
    diff = n[:, None] - n[None, :]
    mask = np.where(diff >= 0, np.exp(log_g[:, None, None] * np.maximum(diff, 0.0)), 0.0)
    q_dec = np.exp(log_g[:, None] * (n + 1.0))
    k_dec = np.exp(log_g[:, None] * (chunk - 1.0 - n))
    c_dec = np.exp(log_g * chunk)
    return (jnp.asarray(mask, F32),
            jnp.asarray(np.broadcast_to(q_dec[:, :, None], (RET_HEADS, chunk, RET_V_DIM)), F32),
            jnp.asarray(np.broadcast_to(k_dec[:, :, None], (RET_HEADS, chunk, RET_QK_DIM)), F32),
            jnp.asarray(np.broadcast_to(c_dec[:, None, None], (RET_HEADS, 1, RET_V_DIM)), F32))


def _retention(rqk, rest, batch, seq, chunk):
    consts = _retention_constants(chunk)
    rqk3 = rqk.reshape(batch, seq, 2 * RET_QK_WIDTH)
    rest3 = rest.reshape(batch, seq, rest.shape[1])
    out = pl.pallas_call(
        _retention_kernel,
        grid=(batch, seq // chunk),
        in_specs=[pl.BlockSpec((1, chunk, RET_QK_WIDTH), lambda b, c: (b, c, 0)),
                  pl.BlockSpec((1, chunk, RET_QK_WIDTH), lambda b, c: (b, c, 1)),
                  pl.BlockSpec((1, chunk, RET_V_WIDTH), lambda b, c: (b, c, 0)),
                  pl.BlockSpec((1, chunk, RET_V_WIDTH), lambda b, c: (b, c, 1)),
                  *[_full(a) for a in consts]],
        out_specs=pl.BlockSpec((1, chunk, RET_V_WIDTH), lambda b, c: (b, c, 0)),
        out_shape=jax.ShapeDtypeStruct((batch, seq, RET_V_WIDTH), BF16),
        scratch_shapes=[pltpu.VMEM((RET_HEADS, RET_QK_DIM, RET_V_DIM), F32)],
        compiler_params=_params("parallel", "arbitrary"),
        name="retention",
    )(rqk3, rqk3, rest3, rest3, *consts)
    return out.reshape(batch * seq, RET_V_WIDTH)


def _layer_norm(y, g, b):
    mu = jnp.mean(y, axis=-1, keepdims=True)
    cen = y - mu
    var = jnp.mean(cen * cen, axis=-1, keepdims=True)
    return cen * lax.rsqrt(var + LN_EPS) * g + b


def _to_token_order(dst_ref, src_ref):
    dil, sub = src_ref.shape[1], src_ref.shape[2]
    chunks = dst_ref.shape[0]
    for r in range(dil):
        for c in range(chunks):
            dst_ref[c, pl.ds(r, sub, stride=dil), :] = src_ref[0, r, :, c * LANES:(c + 1) * LANES].astype(F32)
    return jnp.concatenate([dst_ref[c] for c in range(chunks)], axis=1) if chunks > 1 else dst_ref[0]


def _merge_kernel(o0_ref, o1_ref, o2_ref, l0_ref, l1_ref, l2_ref, yb_ref, gates_ref, x_ref,
                  expand_ref, wa_ref, wr_ref, wo_ref, g_ref, b_ref, out_ref,
                  o1_tok, o2_tok, l1_tok, l2_tok):
    outs = (o0_ref[0, 0].astype(F32), _to_token_order(o1_tok, o1_ref), _to_token_order(o2_tok, o2_ref))
    lses = (l0_ref[0, 0], _to_token_order(l1_tok, l1_ref), _to_token_order(l2_tok, l2_ref))
    m = jnp.maximum(jnp.maximum(lses[0], lses[1]), lses[2])
    es = [jnp.exp(l - m) for l in lses]
    inv = 1.0 / (es[0] + es[1] + es[2])
    expand = expand_ref[...]
    y_a = jnp.zeros(outs[0].shape, F32)
    for e, o in zip(es, outs):
        wgt = e * inv
        hi = wgt.astype(BF16)
        lo = (wgt - hi.astype(F32)).astype(BF16)
        wide = (jnp.dot(hi, expand, preferred_element_type=F32)
                + jnp.dot(lo, expand, preferred_element_type=F32))
        y_a = y_a + wide * o
    gates = gates_ref[...].astype(F32)
    branch_a = jnp.dot(y_a.astype(BF16), wa_ref[...], preferred_element_type=F32)
    branch_b = jnp.dot(yb_ref[...], wr_ref[...], preferred_element_type=F32)
    merged = gates[:, :D_MODEL] * branch_a + gates[:, D_MODEL:] * branch_b
    mix = jnp.dot(merged.astype(BF16), wo_ref[...], preferred_element_type=F32)
    out_ref[...] = _layer_norm(ALPHA * x_ref[...] + mix, g_ref[...], b_ref[...])


def _merge(os_, lses, y_b, rest, x, wa, wr, wo, g, b, seq, tm=512):
    n = x.shape[0]
    tiles = seq // tm
    expand = jnp.asarray(
        (np.arange(LANES)[:, None] == (np.arange(GROUP_WIDTH)[None, :] // HEAD_DIM)).astype(np.float32), BF16)
    gate_block = (2 * RET_V_WIDTH) // (2 * D_MODEL)
    row = lambda width: pl.BlockSpec((tm, width), lambda i: (i, 0))

    def strided(a):
        dil, width = a.shape[1], a.shape[3]
        return pl.BlockSpec((1, dil, tm // dil, width), lambda i: (i // tiles, 0, i % tiles, 0))

    return pl.pallas_call(
        _merge_kernel,
        grid=(n // tm,),
        in_specs=[strided(a) for a in os_] + [strided(a) for a in lses]
                 + [row(RET_V_WIDTH), pl.BlockSpec((tm, 2 * D_MODEL), lambda i: (i, gate_block)), row(D_MODEL),
                    _full(expand), _full(wa), _full(wr), _full(wo), _full(g), _full(b)],
        out_specs=row(D_MODEL),
        out_shape=jax.ShapeDtypeStruct((n, D_MODEL), F32),
        scratch_shapes=[pltpu.VMEM((GROUP_WIDTH // LANES, tm, LANES), F32),
                        pltpu.VMEM((GROUP_WIDTH // LANES, tm, LANES), F32),
                        pltpu.VMEM((1, tm, LANES), F32), pltpu.VMEM((1, tm, LANES), F32)],
        compiler_params=_params("parallel"),
        name="merge",
    )(*os_, *lses, y_b, rest, x, expand, wa, wr, wo, g, b)


def _ffn_kernel(x_ref, wg_ref, wu_ref, wd_ref, g_ref, b_ref, out_ref):
    x = x_ref[...]
    xb = x.astype(BF16)
    gate = jnp.dot(xb, wg_ref[...], preferred_element_type=F32)
    up = jnp.dot(xb, wu_ref[...], preferred_element_type=F32)
    hidden = (gate * jax.nn.sigmoid(gate) * up).astype(BF16)
    ffn = jnp.dot(hidden, wd_ref[...], preferred_element_type=F32)
    out_ref[...] = _layer_norm(ALPHA * x + ffn, g_ref[...], b_ref[...])


def _ffn(x, wg, wu, wd, g, b, tm=256):
    n = x.shape[0]
    row = pl.BlockSpec((tm, D_MODEL), lambda i: (i, 0))
    return pl.pallas_call(
        _ffn_kernel,
        grid=(n // tm,),
        in_specs=[row, _full(wg), _full(wu), _full(wd), _full(g), _full(b)],
        out_specs=row,
        out_shape=jax.ShapeDtypeStruct((n, D_MODEL), F32),
        compiler_params=_params("parallel"),
        name="ffn",
    )(x, wg, wu, wd, g, b)


def _rope_tables(seq):
    half = RET_QK_DIM // 2
    inv_freq = ROPE_BASE ** (-np.arange(half, dtype=np.float64) / half)
    ang = np.arange(seq, dtype=np.float64)[:, None] * inv_freq[None]
    return jnp.asarray(np.cos(ang), F32), jnp.asarray(np.sin(ang), F32)


def _group_major(a):
    parts = [a[..., (t * N_GROUPS + gi) * GROUP_WIDTH:(t * N_GROUPS + gi + 1) * GROUP_WIDTH]
             for gi in range(N_GROUPS) for t in range(3)]
    return jnp.concatenate(parts, axis=-1)


def kernel(x, rel_bias, w_in, b_in, w_attn_proj, w_ret_proj, w_out, ln1_g, ln1_b,
           w_ffn_gate, w_ffn_up, w_ffn_down, ln2_g, ln2_b):
    batch, seq, d = x.shape
    n = batch * seq
    assert d == D_MODEL and seq % 2048 == 0
    cos, sin = _rope_tables(seq)
    biases = [_attention_bias(rel_bias[:, gi * HEADS_PER_GROUP:(gi + 1) * HEADS_PER_GROUP], dil)
              for gi, (_, dil) in enumerate(ATTN_GROUPS)]
    xf = x.reshape(n, d)
    tm = 1024
    pos_blocks = seq // tm
    for l in range(DEPTH):
        w = w_in[l].astype(BF16)
        bias_in = b_in[l]
        qkvs = _project_attn(xf, _group_major(w[:, :_ATTN_COLS]),
                             _group_major(bias_in[:_ATTN_COLS]).reshape(1, _ATTN_COLS), batch, seq)
        rot_spec = pl.BlockSpec((tm, RET_QK_DIM // 2), lambda i, j: (i % pos_blocks, 0))
        rqk = _project(_proj_rot_kernel, "proj_rot", xf, w[:, _COL_RQK[0]:_COL_RQK[1]],
                       bias_in[_COL_RQK[0]:_COL_RQK[1]], tn=RET_QK_WIDTH,
                       extra=(cos, sin), extra_specs=(rot_spec, rot_spec), tm=tm)
        tn_rest = 1024
        rest = _project(functools.partial(_proj_rest_kernel, tiles_per_section=RET_V_WIDTH // tn_rest),
                        "proj_rest", xf, w[:, _COL_REST[0]:_COL_REST[1]], bias_in[_COL_REST[0]:_COL_REST[1]],
                        tn=tn_rest, tm=tm)
        os_, lses = [], []
        for qkv, bias in zip(qkvs, biases):
            o, lse = _attention_group(qkv, bias, q_rows=512)
            os_.append(o)
            lses.append(lse)
        y_b = _retention(rqk, rest, batch, seq, RET_CHUNK)
        xf = _merge(os_, lses, y_b, rest, xf,
                    w_attn_proj[l].astype(BF16), w_ret_proj[l].astype(BF16), w_out[l].astype(BF16),
                    ln1_g[l].reshape(1, d), ln1_b[l].reshape(1, d), seq)
        xf = _ffn(xf, w_ffn_gate[l].astype(BF16), w_ffn_up[l].astype(BF16), w_ffn_down[l].astype(BF16),
                  ln2_g[l].reshape(1, d), ln2_b[l].reshape(1, d))
    return xf.reshape(batch, seq, d)
```

```python
import functools

import numpy as np
import jax
import jax.numpy as jnp
from jax import lax
from jax.experimental import pallas as pl
from jax.experimental.pallas import tpu as pltpu

F32 = jnp.float32
BF16 = jnp.bfloat16

D_MODEL = 1024
DEPTH = 2
HEAD_DIM = 64
ATTN_GROUPS = ((128, 1), (512, 4), (2048, 16))
N_GROUPS = len(ATTN_GROUPS)
HEADS_PER_GROUP = 6
GROUP_WIDTH = HEADS_PER_GROUP * HEAD_DIM
QKV_WIDTH = 3 * GROUP_WIDTH
NUM_BUCKETS = 32
MAX_DISTANCE = 2048
ATTN_BLOCK = 128
LANES = 128
RET_HEADS = 4
RET_QK_DIM = 256
RET_V_DIM = 512
RET_QK_WIDTH = RET_HEADS * RET_QK_DIM
RET_V_WIDTH = RET_HEADS * RET_V_DIM
RET_CHUNK = 256
ROPE_BASE = 10000.0
D_FF = 2816
ALPHA = (2 * DEPTH) ** 0.25
LN_EPS = 1e-5
GN_EPS = 1e-5
MASK_VALUE = -1e30

_ATTN_COLS = 3 * N_GROUPS * GROUP_WIDTH
_COL_RQK = (_ATTN_COLS, _ATTN_COLS + 2 * RET_QK_WIDTH)
_COL_REST = (_COL_RQK[1], _COL_RQK[1] + 2 * RET_V_WIDTH + 2 * D_MODEL)

VMEM_LIMIT = 56 * 1024 * 1024


def _params(*sem):
    return pltpu.CompilerParams(dimension_semantics=sem, vmem_limit_bytes=VMEM_LIMIT)


def _full(a):
    return pl.BlockSpec(a.shape, lambda *_: (0,) * a.ndim)


def _sigmoid(x):
    return 0.5 * jnp.tanh(0.5 * x) + 0.5


_REST_CHUNK = 512


def _proj_kernel(x_ref, wa_ref, wq_ref, wr_ref, ba_ref, bq_ref, br_ref, cos_ref, sin_ref,
                 o0_ref, o1_ref, o2_ref, rqk_ref, rest_ref, stage_ref):
    tm = x_ref.shape[0]
    xb = x_ref[...].astype(BF16)

    half = RET_QK_DIM // 2
    for t, scale in ((0, 1.0), (1, RET_QK_DIM ** -0.5)):
        c = cos_ref[...] * scale
        s = sin_ref[...] * scale
        for h in range(RET_HEADS):
            lo = t * RET_QK_WIDTH + h * RET_QK_DIM
            acc = (jnp.dot(xb, wq_ref[:, lo:lo + RET_QK_DIM], preferred_element_type=F32)
                   + bq_ref[:, lo:lo + RET_QK_DIM])
            t1 = acc[:, :half]
            t2 = acc[:, half:]
            rqk_ref[:, lo:lo + half] = (t1 * c - t2 * s).astype(rqk_ref.dtype)
            rqk_ref[:, lo + half:lo + RET_QK_DIM] = (t1 * s + t2 * c).astype(rqk_ref.dtype)

    for lo in range(0, wr_ref.shape[1], _REST_CHUNK):
        acc = (jnp.dot(xb, wr_ref[:, lo:lo + _REST_CHUNK], preferred_element_type=F32)
               + br_ref[:, lo:lo + _REST_CHUNK])
        if lo >= 2 * RET_V_WIDTH:
            acc = _sigmoid(acc)
        elif lo >= RET_V_WIDTH:
            acc = acc * _sigmoid(acc)
        rest_ref[:, lo:lo + _REST_CHUNK] = acc.astype(rest_ref.dtype)

    for gi, ((_, dil), o_ref) in enumerate(zip(ATTN_GROUPS, (o0_ref, o1_ref, o2_ref))):
        sub = tm // dil
        cols = slice(gi * QKV_WIDTH, (gi + 1) * QKV_WIDTH)
        acc = jnp.dot(xb, wa_ref[:, cols], preferred_element_type=F32) + ba_ref[:, cols]
        q = acc[:, :GROUP_WIDTH] * HEAD_DIM ** -0.5
        if dil == 1:
            o_ref[0, 0, :, :GROUP_WIDTH] = q.astype(o_ref.dtype)
            o_ref[0, 0, :, GROUP_WIDTH:] = acc[:, GROUP_WIDTH:].astype(o_ref.dtype)
            continue
        for c in range(QKV_WIDTH // LANES):
            lanes = slice(c * LANES, (c + 1) * LANES)
            stage_ref[c] = q[:, lanes] if c < GROUP_WIDTH // LANES else acc[:, lanes]
        for r in range(dil):
            for c in range(QKV_WIDTH // LANES):
                o_ref[0, r, :, c * LANES:(c + 1) * LANES] = (
                    stage_ref[c, pl.ds(r, sub, stride=dil), :].astype(o_ref.dtype))


def _project(x, wa, wq, wr, ba, bq, br, cos, sin, batch, seq, tm=256):
    n, d = x.shape
    tiles = seq // tm
    outs, specs = [], []
    for _, dil in ATTN_GROUPS:
        outs.append(jax.ShapeDtypeStruct((batch, dil, seq // dil, QKV_WIDTH), BF16))
        specs.append(pl.BlockSpec((1, dil, tm // dil, QKV_WIDTH), lambda i: (i // tiles, 0, i % tiles, 0)))
    for w in (wq, wr):
        outs.append(jax.ShapeDtypeStruct((n, w.shape[1]), BF16))
        specs.append(pl.BlockSpec((tm, w.shape[1]), lambda i: (i, 0)))
    resident = lambda a: pl.BlockSpec(a.shape, lambda i: (0,) * a.ndim, pipeline_mode=pl.Buffered(1))
    rot_spec = pl.BlockSpec((tm, RET_QK_DIM // 2), lambda i: (i % tiles, 0))
    return pl.pallas_call(
        _proj_kernel,
        grid=(n // tm,),
        in_specs=[pl.BlockSpec((tm, d), lambda i: (i, 0)), resident(wa), resident(wq), resident(wr),
                  resident(ba), resident(bq), resident(br), rot_spec, rot_spec],
        out_specs=specs,
        out_shape=outs,
        scratch_shapes=[pltpu.VMEM((QKV_WIDTH // LANES, tm, LANES), F32)],
        compiler_params=_params("parallel"),
        name="proj",
    )(x, wa, wq, wr, ba, bq, br, cos, sin)


def _attn_kernel(q_ref, kp_ref, kc_ref, vp_ref, vc_ref, bias_ref, o_ref, lse_ref, *, sub_blocks):
    first_step = pl.program_id(2) == 0
    w = ATTN_BLOCK
    lane = lax.broadcasted_iota(jnp.int32, (w, LANES), 1)
    low_half = lane < HEAD_DIM
    ones = jnp.ones((2 * w, LANES), BF16)
    for j in range(sub_blocks):
        rows = slice(j * w, (j + 1) * w)
        lse_tile = jnp.zeros((w, LANES), F32)
        for pair in range(HEADS_PER_GROUP // 2):
            cols = slice(pair * LANES, (pair + 1) * LANES)
            q = q_ref[0, 0, rows, cols]
            if j == 0:
                k = jnp.concatenate([kp_ref[0, 0, :, cols], kc_ref[0, 0, :w, cols]], axis=0)
                v = jnp.concatenate([vp_ref[0, 0, :, cols], vc_ref[0, 0, :w, cols]], axis=0)
            else:
                k = kc_ref[0, 0, (j - 1) * w:(j + 1) * w, cols]
                v = vc_ref[0, 0, (j - 1) * w:(j + 1) * w, cols]
            v_ones = jnp.concatenate([v, ones], axis=1)
            halves = []
            for sel, hh in ((low_half, 0), (~low_half, 1)):
                h = 2 * pair + hh
                bias = bias_ref[jnp.where(first_step, 0, 1), h] if j == 0 else bias_ref[1, h]
                qh = jnp.where(sel, q, jnp.zeros_like(q))
                s = lax.dot_general(qh, k, (((1,), (1,)), ((), ())), preferred_element_type=F32) + bias
                m = jnp.max(s, axis=-1, keepdims=True)
                p = jnp.exp(s - m).astype(BF16)
                ov = jnp.dot(p, v_ones, preferred_element_type=F32)
                denom = ov[:, LANES:]
                halves.append(ov[:, :LANES] / denom)
                lse_tile = jnp.where(lane == h, m + jnp.log(denom), lse_tile)
            o_ref[0, 0, rows, cols] = jnp.where(low_half, halves[0], halves[1]).astype(o_ref.dtype)
        lse_ref[0, 0, rows, :] = lse_tile


def _attention_group(qkv, bias, q_rows):
    batch, dilation, sub_len, _ = qkv.shape
    q_rows = min(q_rows, sub_len)
    sub_blocks = q_rows // ATTN_BLOCK
    steps = sub_len // q_rows

    def cur(unit):
        return pl.BlockSpec((1, 1, q_rows, GROUP_WIDTH), lambda b, r, n: (b, r, n, unit))

    def prev(unit):
        return pl.BlockSpec((1, 1, ATTN_BLOCK, GROUP_WIDTH),
                            lambda b, r, n: (b, r, jnp.maximum(n * sub_blocks - 1, 0), unit))

    return pl.pallas_call(
        functools.partial(_attn_kernel, sub_blocks=sub_blocks),
        grid=(batch, dilation, steps),
        in_specs=[cur(0), prev(1), cur(1), prev(2), cur(2), _full(bias)],
        out_specs=[pl.BlockSpec((1, 1, q_rows, GROUP_WIDTH), lambda b, r, n: (b, r, n, 0)),
                   pl.BlockSpec((1, 1, q_rows, LANES), lambda b, r, n: (b, r, n, 0))],
        out_shape=[jax.ShapeDtypeStruct((batch, dilation, sub_len, GROUP_WIDTH), BF16),
                   jax.ShapeDtypeStruct((batch, dilation, sub_len, LANES), F32)],
        compiler_params=_params("parallel", "parallel", "arbitrary"),
        name=f"attn_d{dilation}",
    )(qkv, qkv, qkv, qkv, qkv, bias)


def _t5_bucket(dist):
    max_exact = NUM_BUCKETS // 2
    large = max_exact + (np.log(np.maximum(dist, max_exact) / max_exact)
                         / np.log(MAX_DISTANCE / max_exact)
                         * (NUM_BUCKETS - max_exact)).astype(np.int32)
    large = np.minimum(large, NUM_BUCKETS - 1)
    return np.where(dist < max_exact, dist, large).astype(np.int32)


def _attention_bias(rel_bias_group, dilation):
    w = ATTN_BLOCK
    buckets = _t5_bucket((w - np.arange(w + 1)) * dilation)
    onehot = jnp.asarray(np.eye(NUM_BUCKETS, dtype=np.float32)[buckets])
    per_c = jnp.einsum('cb,bh->hc', onehot, rel_bias_group.astype(F32),
                       precision=lax.Precision.HIGHEST)
    period = 2 * w + 1
    vec = jnp.concatenate([per_c, jnp.full((HEADS_PER_GROUP, period - (w + 1)), MASK_VALUE, F32)], axis=1)
    normal = jnp.tile(vec, (1, w))[:, :w * 2 * w].reshape(HEADS_PER_GROUP, w, 2 * w)
    first = jnp.where(np.arange(2 * w)[None, None, :] >= w, normal, MASK_VALUE)
    return jnp.stack([first, normal], axis=0)


def _retention_kernel(q_ref, k_ref, v_ref, g_ref, mask_ref, qdec_ref, kdec_ref, cdec_ref, o_ref, state_ref):
    @pl.when(pl.program_id(1) == 0)
    def _():
        state_ref[...] = jnp.zeros_like(state_ref)

    for h in range(RET_HEADS):
        qk_cols = slice(h * RET_QK_DIM, (h + 1) * RET_QK_DIM)
        v_cols = slice(h * RET_V_DIM, (h + 1) * RET_V_DIM)
        q = q_ref[0, :, qk_cols]
        k = k_ref[0, :, qk_cols]
        v = v_ref[0, :, v_cols]
        state = state_ref[h]
        sc = lax.dot_general(q, k, (((1,), (1,)), ((), ())), preferred_element_type=F32) * mask_ref[h]
        inner = jnp.dot(sc.astype(BF16), v, preferred_element_type=F32)
        cross = jnp.dot(q, state.astype(BF16), preferred_element_type=F32) * qdec_ref[h]
        o = inner + cross
        kd = (k.astype(F32) * kdec_ref[h]).astype(BF16)
        state_ref[h] = state * cdec_ref[h] + lax.dot_general(
            kd, v, (((0,), (0,)), ((), ())), preferred_element_type=F32)
        mu = jnp.mean(o, axis=-1, keepdims=True)
        cen = o - mu
        var = jnp.mean(cen * cen, axis=-1, keepdims=True)
        y = cen * lax.rsqrt(var + GN_EPS)
        o_ref[0, :, v_cols] = (g_ref[0, :, v_cols].astype(F32) * y).astype(o_ref.dtype)


def _retention_constants(chunk):
    log_g = np.log(1.0 - 2.0 ** (-5.0 - np.arange(RET_HEADS, dtype=np.float64)))
    n = np.arange(chunk, dtype=np.float64)
    diff = n[:, None] - n[None, :]
    mask = np.where(diff >= 0, np.exp(log_g[:, None, None] * np.maximum(diff, 0.0)), 0.0)
    q_dec = np.exp(log_g[:, None] * (n + 1.0))
    k_dec = np.exp(log_g[:, None] * (chunk - 1.0 - n))
    c_dec = np.exp(log_g * chunk)
    return (jnp.asarray(mask, F32),
            jnp.asarray(np.broadcast_to(q_dec[:, :, None], (RET_HEADS, chunk, RET_V_DIM)), F32),
            jnp.asarray(np.broadcast_to(k_dec[:, :, None], (RET_HEADS, chunk, RET_QK_DIM)), F32),
            jnp.asarray(np.broadcast_to(c_dec[:, None, None], (RET_HEADS, 1, RET_V_DIM)), F32))


def _retention(rqk, rest, batch, seq, chunk):
    consts = _retention_constants(chunk)
    rqk3 = rqk.reshape(batch, seq, 2 * RET_QK_WIDTH)
    rest3 = rest.reshape(batch, seq, rest.shape[1])
    out = pl.pallas_call(
        _retention_kernel,
        grid=(batch, seq // chunk),
        in_specs=[pl.BlockSpec((1, chunk, RET_QK_WIDTH), lambda b, c: (b, c, 0)),
                  pl.BlockSpec((1, chunk, RET_QK_WIDTH), lambda b, c: (b, c, 1)),
                  pl.BlockSpec((1, chunk, RET_V_WIDTH), lambda b, c: (b, c, 0)),
                  pl.BlockSpec((1, chunk, RET_V_WIDTH), lambda b, c: (b, c, 1)),
                  *[_full(a) for a in consts]],
        out_specs=pl.BlockSpec((1, chunk, RET_V_WIDTH), lambda b, c: (b, c, 0)),
        out_shape=jax.ShapeDtypeStruct((batch, seq, RET_V_WIDTH), BF16),
        scratch_shapes=[pltpu.VMEM((RET_HEADS, RET_QK_DIM, RET_V_DIM), F32)],
        compiler_params=_params("parallel", "arbitrary"),
        name="retention",
    )(rqk3, rqk3, rest3, rest3, *consts)
    return out.reshape(batch * seq, RET_V_WIDTH)


def _layer_norm(y, g, b):
    mu = jnp.mean(y, axis=-1, keepdims=True)
    cen = y - mu
    var = jnp.mean(cen * cen, axis=-1, keepdims=True)
    return cen * lax.rsqrt(var + LN_EPS) * g + b


def _to_token_order(dst_ref, src_ref):
    dil, sub = src_ref.shape[1], src_ref.shape[2]
    chunks = dst_ref.shape[0]
    for r in range(dil):
        for c in range(chunks):
            dst_ref[c, pl.ds(r, sub, stride=dil), :] = src_ref[0, r, :, c * LANES:(c + 1) * LANES].astype(F32)
    return jnp.concatenate([dst_ref[c] for c in range(chunks)], axis=1) if chunks > 1 else dst_ref[0]


def _merge_kernel(o0_ref, o1_ref, o2_ref, l0_ref, l1_ref, l2_ref, yb_ref, gates_ref, x_ref,
                  expand_ref, wa_ref, wr_ref, wo_ref, g_ref, b_ref, out_ref,
                  o1_tok, o2_tok, l1_tok, l2_tok):
    outs = (o0_ref[0, 0].astype(F32), _to_token_order(o1_tok, o1_ref), _to_token_order(o2_tok, o2_ref))
    lses = (l0_ref[0, 0], _to_token_order(l1_tok, l1_ref), _to_token_order(l2_tok, l2_ref))
    m = jnp.maximum(jnp.maximum(lses[0], lses[1]), lses[2])
    es = [jnp.exp(l - m) for l in lses]
    inv = 1.0 / (es[0] + es[1] + es[2])
    expand = expand_ref[...]
    y_a = jnp.zeros(outs[0].shape, F32)
    for e, o in zip(es, outs):
        wgt = e * inv
        hi = wgt.astype(BF16)
        lo = (wgt - hi.astype(F32)).astype(BF16)
        wide = jnp.dot(jnp.concatenate([hi, lo], axis=1), expand, preferred_element_type=F32)
        y_a = y_a + wide * o
    gates = gates_ref[...].astype(F32)
    branch_a = jnp.dot(y_a.astype(BF16), wa_ref[...], preferred_element_type=F32)
    branch_b = jnp.dot(yb_ref[...], wr_ref[...], preferred_element_type=F32)
    merged = gates[:, :D_MODEL] * branch_a + gates[:, D_MODEL:] * branch_b
    mix = jnp.dot(merged.astype(BF16), wo_ref[...], preferred_element_type=F32)
    out_ref[...] = _layer_norm(ALPHA * x_ref[...] + mix, g_ref[...], b_ref[...])


def _merge(os_, lses, y_b, rest, x, wa, wr, wo, g, b, seq, tm=512):
    n = x.shape[0]
    tiles = seq // tm
    expand = jnp.asarray(
        (np.arange(2 * LANES)[:, None] % LANES == (np.arange(GROUP_WIDTH)[None, :] // HEAD_DIM)).astype(np.float32),
        BF16)
    gate_block = (2 * RET_V_WIDTH) // (2 * D_MODEL)
    row = lambda width: pl.BlockSpec((tm, width), lambda i: (i, 0))

    def strided(a):
        dil, width = a.shape[1], a.shape[3]
        return pl.BlockSpec((1, dil, tm // dil, width), lambda i: (i // tiles, 0, i % tiles, 0))

    return pl.pallas_call(
        _merge_kernel,
        grid=(n // tm,),
        in_specs=[strided(a) for a in os_] + [strided(a) for a in lses]
                 + [row(RET_V_WIDTH), pl.BlockSpec((tm, 2 * D_MODEL), lambda i: (i, gate_block)), row(D_MODEL),
                    _full(expand), _full(wa), _full(wr), _full(wo), _full(g), _full(b)],
        out_specs=row(D_MODEL),
        out_shape=jax.ShapeDtypeStruct((n, D_MODEL), F32),
        scratch_shapes=[pltpu.VMEM((GROUP_WIDTH // LANES, tm, LANES), F32),
                        pltpu.VMEM((GROUP_WIDTH // LANES, tm, LANES), F32),
                        pltpu.VMEM((1, tm, LANES), F32), pltpu.VMEM((1, tm, LANES), F32)],
        compiler_params=_params("parallel"),
        name="merge",
    )(*os_, *lses, y_b, rest, x, expand, wa, wr, wo, g, b)


def _ffn_kernel(x_ref, wg_ref, wu_ref, wd_ref, g_ref, b_ref, out_ref):
    x = x_ref[...]
    xb = x.astype(BF16)
    gate = jnp.dot(xb, wg_ref[...], preferred_element_type=F32)
    up = jnp.dot(xb, wu_ref[...], preferred_element_type=F32)
    hidden = (gate * _sigmoid(gate) * up).astype(BF16)
    ffn = jnp.dot(hidden, wd_ref[...], preferred_element_type=F32)
    out_ref[...] = _layer_norm(ALPHA * x + ffn, g_ref[...], b_ref[...])


def _ffn(x, wg, wu, wd, g, b, tm=256):
    n = x.shape[0]
    row = pl.BlockSpec((tm, D_MODEL), lambda i: (i, 0))
    return pl.pallas_call(
        _ffn_kernel,
        grid=(n // tm,),
        in_specs=[row, _full(wg), _full(wu), _full(wd), _full(g), _full(b)],
        out_specs=row,
        out_shape=jax.ShapeDtypeStruct((n, D_MODEL), F32),
        compiler_params=_params("parallel"),
        name="ffn",
    )(x, wg, wu, wd, g, b)


def _rope_tables(seq):
    half = RET_QK_DIM // 2
    inv_freq = ROPE_BASE ** (-np.arange(half, dtype=np.float64) / half)
    ang = np.arange(seq, dtype=np.float64)[:, None] * inv_freq[None]
    return jnp.asarray(np.cos(ang), F32), jnp.asarray(np.sin(ang), F32)


def _group_major(a):
    parts = [a[..., (t * N_GROUPS + gi) * GROUP_WIDTH:(t * N_GROUPS + gi + 1) * GROUP_WIDTH]
             for gi in range(N_GROUPS) for t in range(3)]
    return jnp.concatenate(parts, axis=-1)


def kernel(x, rel_bias, w_in, b_in, w_attn_proj, w_ret_proj, w_out, ln1_g, ln1_b,
           w_ffn_gate, w_ffn_up, w_ffn_down, ln2_g, ln2_b):
    batch, seq, d = x.shape
    n = batch * seq
    assert d == D_MODEL and seq % 2048 == 0
    cos, sin = _rope_tables(seq)
    biases = [_attention_bias(rel_bias[:, gi * HEADS_PER_GROUP:(gi + 1) * HEADS_PER_GROUP], dil)
              for gi, (_, dil) in enumerate(ATTN_GROUPS)]
    xf = x.reshape(n, d)
    for l in range(DEPTH):
        w = w_in[l].astype(BF16)
        bias_in = b_in[l].reshape(1, -1)
        *qkvs, rqk, rest = _project(
            xf, _group_major(w[:, :_ATTN_COLS]), w[:, _COL_RQK[0]:_COL_RQK[1]], w[:, _COL_REST[0]:_COL_REST[1]],
            _group_major(bias_in[:, :_ATTN_COLS]), bias_in[:, _COL_RQK[0]:_COL_RQK[1]],
            bias_in[:, _COL_REST[0]:_COL_REST[1]], cos, sin, batch, seq)
        os_, lses = [], []
        for qkv, bias in zip(qkvs, biases):
            o, lse = _attention_group(qkv, bias, q_rows=512)
            os_.append(o)
            lses.append(lse)
        y_b = _retention(rqk, rest, batch, seq, RET_CHUNK)
        xf = _merge(os_, lses, y_b, rest, xf,
                    w_attn_proj[l].astype(BF16), w_ret_proj[l].astype(BF16), w_out[l].astype(BF16),
                    ln1_g[l].reshape(1, d), ln1_b[l].reshape(1, d), seq)
        xf = _ffn(xf, w_ffn_gate[l].astype(BF16), w_ffn_up[l].astype(BF16), w_ffn_down[l].astype(BF16),
                  ln2_g[l].reshape(1, d), ln2_b[l].reshape(1, d))
    return xf.reshape(batch, seq, d)
```

```python
import functools

import numpy as np
import jax
import jax.numpy as jnp
from jax import lax
from jax.experimental import pallas as pl
from jax.experimental.pallas import tpu as pltpu

F32 = jnp.float32
BF16 = jnp.bfloat16

D_MODEL = 1024
DEPTH = 2
HEAD_DIM = 64
ATTN_GROUPS = ((128, 1), (512, 4), (2048, 16))
N_GROUPS = len(ATTN_GROUPS)
HEADS_PER_GROUP = 6
GROUP_WIDTH = HEADS_PER_GROUP * HEAD_DIM
QKV_WIDTH = 3 * GROUP_WIDTH
NUM_BUCKETS = 32
MAX_DISTANCE = 2048
ATTN_BLOCK = 128
LANES = 128
RET_HEADS = 4
RET_QK_DIM = 256
RET_V_DIM = 512
RET_QK_WIDTH = RET_HEADS * RET_QK_DIM
RET_V_WIDTH = RET_HEADS * RET_V_DIM
RET_CHUNK = 256
ROPE_BASE = 10000.0
D_FF = 2816
ALPHA = (2 * DEPTH) ** 0.25
LN_EPS = 1e-5
GN_EPS = 1e-5
MASK_VALUE = -1e30
LOG2E = 1.4426950408889634
LN2 = 0.6931471805599453

_ATTN_COLS = 3 * N_GROUPS * GROUP_WIDTH
_COL_RQK = (_ATTN_COLS, _ATTN_COLS + 2 * RET_QK_WIDTH)
_COL_REST = (_COL_RQK[1], _COL_RQK[1] + 2 * RET_V_WIDTH + 2 * D_MODEL)
_REST_CHUNK = 512

MERGE_SUBTILES = 1
FFN_SUBTILES = 2

VMEM_LIMIT = 56 * 1024 * 1024


def _params(*sem):
    return pltpu.CompilerParams(dimension_semantics=sem, vmem_limit_bytes=VMEM_LIMIT)


def _full(a):
    return pl.BlockSpec(a.shape, lambda *_: (0,) * a.ndim)


def _resident(a):
    return pl.BlockSpec(a.shape, lambda *_: (0,) * a.ndim, pipeline_mode=pl.Buffered(1))


def _sigmoid(x):
    return 0.5 * jnp.tanh(0.5 * x) + 0.5


def _proj_kernel(x_ref, wa_ref, wq_ref, wr_ref, ba_ref, bq_ref, br_ref, cos_ref, sin_ref,
                 o0_ref, o1_ref, o2_ref, rqk_ref, rest_ref, xb_ref, stage_ref):
    tm = x_ref.shape[0]
    xb_ref[...] = x_ref[...].astype(BF16)

    for gi, ((_, dil), o_ref) in reversed(list(enumerate(zip(ATTN_GROUPS, (o0_ref, o1_ref, o2_ref))))):
        sub = tm // dil
        cols = slice(gi * QKV_WIDTH, (gi + 1) * QKV_WIDTH)
        acc = jnp.dot(xb_ref[...], wa_ref[:, cols], preferred_element_type=F32) + ba_ref[:, cols]
        q = acc[:, :GROUP_WIDTH] * (HEAD_DIM ** -0.5 * LOG2E)
        if dil == 1:
            o_ref[0, 0, :, :GROUP_WIDTH] = q.astype(o_ref.dtype)
            o_ref[0, 0, :, GROUP_WIDTH:] = acc[:, GROUP_WIDTH:].astype(o_ref.dtype)
            continue
        for c in range(QKV_WIDTH // LANES):
            lanes = slice(c * LANES, (c + 1) * LANES)
            stage_ref[c] = q[:, lanes] if c < GROUP_WIDTH // LANES else acc[:, lanes]
        for r in range(dil):
            for c in range(QKV_WIDTH // LANES):
                o_ref[0, r, :, c * LANES:(c + 1) * LANES] = (
                    stage_ref[c, pl.ds(r, sub, stride=dil), :].astype(o_ref.dtype))

    half = RET_QK_DIM // 2
    for t, scale in ((0, 1.0), (1, RET_QK_DIM ** -0.5)):
        c = cos_ref[...] * scale
        s = sin_ref[...] * scale
        for h in range(RET_HEADS):
            lo = t * RET_QK_WIDTH + h * RET_QK_DIM
            acc = (jnp.dot(xb_ref[...], wq_ref[:, lo:lo + RET_QK_DIM], preferred_element_type=F32)
                   + bq_ref[:, lo:lo + RET_QK_DIM])
            t1 = acc[:, :half]
            t2 = acc[:, half:]
            rqk_ref[:, lo:lo + half] = (t1 * c - t2 * s).astype(rqk_ref.dtype)
            rqk_ref[:, lo + half:lo + RET_QK_DIM] = (t1 * s + t2 * c).astype(rqk_ref.dtype)

    for lo in range(0, wr_ref.shape[1], _REST_CHUNK):
        acc = (jnp.dot(xb_ref[...], wr_ref[:, lo:lo + _REST_CHUNK], preferred_element_type=F32)
               + br_ref[:, lo:lo + _REST_CHUNK])
        if lo >= 2 * RET_V_WIDTH:
            acc = _sigmoid(acc)
        elif lo >= RET_V_WIDTH:
            acc = acc * _sigmoid(acc)
        rest_ref[:, lo:lo + _REST_CHUNK] = acc.astype(rest_ref.dtype)


def _project(x, wa, wq, wr, ba, bq, br, cos, sin, batch, seq, tm=256):
    n, d = x.shape
    tiles = seq // tm
    outs, specs = [], []
    for _, dil in ATTN_GROUPS:
        outs.append(jax.ShapeDtypeStruct((batch, dil, seq // dil, QKV_WIDTH), BF16))
        specs.append(pl.BlockSpec((1, dil, tm // dil, QKV_WIDTH), lambda i: (i // tiles, 0, i % tiles, 0)))
    for w in (wq, wr):
        outs.append(jax.ShapeDtypeStruct((n, w.shape[1]), BF16))
        specs.append(pl.BlockSpec((tm, w.shape[1]), lambda i: (i, 0)))
    rot_spec = pl.BlockSpec((tm, RET_QK_DIM // 2), lambda i: (i % tiles, 0))
    return pl.pallas_call(
        _proj_kernel,
        grid=(n // tm,),
        in_specs=[pl.BlockSpec((tm, d), lambda i: (i, 0)), _resident(wa), _resident(wq), _resident(wr),
                  _resident(ba), _resident(bq), _resident(br), rot_spec, rot_spec],
        out_specs=specs,
        out_shape=outs,
        scratch_shapes=[pltpu.VMEM((tm, d), BF16), pltpu.VMEM((QKV_WIDTH // LANES, tm, LANES), F32)],
        compiler_params=_params("parallel"),
        name="proj",
    )(x, wa, wq, wr, ba, bq, br, cos, sin)


def _attn_kernel(q_ref, kp_ref, kc_ref, vp_ref, vc_ref, bias_ref, o_ref, lse_ref, *, sub_blocks):
    first_step = pl.program_id(2) == 0
    w = ATTN_BLOCK
    lane = lax.broadcasted_iota(jnp.int32, (w, LANES), 1)
    low_half = lane < HEAD_DIM
    ones = jnp.ones((2 * w, LANES), BF16)
    for j in range(sub_blocks):
        rows = slice(j * w, (j + 1) * w)
        max_tile = jnp.zeros((w, LANES), F32)
        denom_tile = jnp.ones((w, LANES), F32)
        for pair in range(HEADS_PER_GROUP // 2):
            cols = slice(pair * LANES, (pair + 1) * LANES)
            q = q_ref[0, 0, rows, cols]
            if j == 0:
                k = jnp.concatenate([kp_ref[0, 0, :, cols], kc_ref[0, 0, :w, cols]], axis=0)
                v = jnp.concatenate([vp_ref[0, 0, :, cols], vc_ref[0, 0, :w, cols]], axis=0)
            else:
                k = kc_ref[0, 0, (j - 1) * w:(j + 1) * w, cols]
                v = vc_ref[0, 0, (j - 1) * w:(j + 1) * w, cols]
            v_ones = jnp.concatenate([v, ones], axis=1)
            halves = []
            for sel, hh in ((low_half, 0), (~low_half, 1)):
                h = 2 * pair + hh
                bias = bias_ref[jnp.where(first_step, 0, 1), h] if j == 0 else bias_ref[1, h]
                qh = jnp.where(sel, q, jnp.zeros_like(q))
                s = lax.dot_general(qh, k, (((1,), (1,)), ((), ())), preferred_element_type=F32) + bias
                m = jnp.max(s, axis=-1, keepdims=True)
                p = jnp.exp2(s - m).astype(BF16)
                ov = jnp.dot(p, v_ones, preferred_element_type=F32)
                halves.append(ov)
                max_tile = jnp.where(lane == h, m, max_tile)
                denom_tile = jnp.where(lane == h, ov[:, LANES:], denom_tile)
            numer = jnp.where(low_half, halves[0][:, :LANES], halves[1][:, :LANES])
            denom = jnp.where(low_half, halves[0][:, LANES:], halves[1][:, LANES:])
            o_ref[0, 0, rows, cols] = (numer / denom).astype(o_ref.dtype)
        lse_ref[0, 0, rows, :] = max_tile * LN2 + jnp.log(denom_tile)


def _attention_group(qkv, bias, q_rows):
    batch, dilation, sub_len, _ = qkv.shape
    q_rows = min(q_rows, sub_len)
    sub_blocks = q_rows // ATTN_BLOCK
    steps = sub_len // q_rows

    def cur(unit):
        return pl.BlockSpec((1, 1, q_rows, GROUP_WIDTH), lambda b, r, n: (b, r, n, unit))

    def prev(unit):
        return pl.BlockSpec((1, 1, ATTN_BLOCK, GROUP_WIDTH),
                            lambda b, r, n: (b, r, jnp.maximum(n * sub_blocks - 1, 0), unit))

    return pl.pallas_call(
        functools.partial(_attn_kernel, sub_blocks=sub_blocks),
        grid=(batch, dilation, steps),
        in_specs=[cur(0), prev(1), cur(1), prev(2), cur(2), _full(bias)],
        out_specs=[pl.BlockSpec((1, 1, q_rows, GROUP_WIDTH), lambda b, r, n: (b, r, n, 0)),
                   pl.BlockSpec((1, 1, q_rows, LANES), lambda b, r, n: (b, r, n, 0))],
        out_shape=[jax.ShapeDtypeStruct((batch, dilation, sub_len, GROUP_WIDTH), BF16),
                   jax.ShapeDtypeStruct((batch, dilation, sub_len, LANES), F32)],
        compiler_params=_params("parallel", "parallel", "arbitrary"),
        name=f"attn_d{dilation}",
    )(qkv, qkv, qkv, qkv, qkv, bias)


def _t5_bucket(dist):
    max_exact = NUM_BUCKETS // 2
    large = max_exact + (np.log(np.maximum(dist, max_exact) / max_exact)
                         / np.log(MAX_DISTANCE / max_exact)
                         * (NUM_BUCKETS - max_exact)).astype(np.int32)
    large = np.minimum(large, NUM_BUCKETS - 1)
    return np.where(dist < max_exact, dist, large).astype(np.int32)


def _attention_bias(rel_bias_group, dilation):
    w = ATTN_BLOCK
    buckets = _t5_bucket((w - np.arange(w + 1)) * dilation)
    onehot = jnp.asarray(np.eye(NUM_BUCKETS, dtype=np.float32)[buckets])
    per_c = jnp.einsum('cb,bh->hc', onehot, rel_bias_group.astype(F32),
                       precision=lax.Precision.HIGHEST) * LOG2E
    period = 2 * w + 1
    vec = jnp.concatenate([per_c, jnp.full((HEADS_PER_GROUP, period - (w + 1)), MASK_VALUE, F32)], axis=1)
    normal = jnp.tile(vec, (1, w))[:, :w * 2 * w].reshape(HEADS_PER_GROUP, w, 2 * w)
    first = jnp.where(np.arange(2 * w)[None, None, :] >= w, normal, MASK_VALUE)
    return jnp.stack([first, normal], axis=0)


def _retention_kernel(q_ref, k_ref, v_ref, g_ref, mask_ref, qdec_ref, kdec_ref, cdec_ref, o_ref,
                      state_ref, lhs_ref, acc_ref):
    @pl.when(pl.program_id(1) == 0)
    def _():
        state_ref[...] = jnp.zeros_like(state_ref)

    chunk = q_ref.shape[1]
    qk_cols = [slice(h * RET_QK_DIM, (h + 1) * RET_QK_DIM) for h in range(RET_HEADS)]
    v_cols = [slice(h * RET_V_DIM, (h + 1) * RET_V_DIM) for h in range(RET_HEADS)]
    for h in range(RET_HEADS):
        q = q_ref[0, :, qk_cols[h]]
        k = k_ref[0, :, qk_cols[h]]
        sc = lax.dot_general(q, k, (((1,), (1,)), ((), ())), preferred_element_type=F32) * mask_ref[h]
        lhs_ref[h, :, :chunk] = sc.astype(BF16)
        lhs_ref[h, :, chunk:] = (q.astype(F32) * qdec_ref[h]).astype(BF16)
    for h in range(RET_HEADS):
        rhs = jnp.concatenate([v_ref[0, :, v_cols[h]], state_ref[h].astype(BF16)], axis=0)
        acc_ref[:, v_cols[h]] = jnp.dot(lhs_ref[h], rhs, preferred_element_type=F32)
    for h in range(RET_HEADS):
        kd = (k_ref[0, :, qk_cols[h]].astype(F32) * kdec_ref[h]).astype(BF16)
        state_ref[h] = state_ref[h] * cdec_ref[h] + lax.dot_general(
            kd, v_ref[0, :, v_cols[h]], (((0,), (0,)), ((), ())), preferred_element_type=F32)
    for h in range(RET_HEADS):
        o = acc_ref[:, v_cols[h]]
        mu = jnp.mean(o, axis=-1, keepdims=True)
        cen = o - mu
        var = jnp.mean(cen * cen, axis=-1, keepdims=True)
        y = cen * lax.rsqrt(var + GN_EPS)
        o_ref[0, :, v_cols[h]] = g_ref[0, :, v_cols[h]] * y.astype(o_ref.dtype)


def _retention_constants(chunk):
    log_g = np.log(1.0 - 2.0 ** (-5.0 - np.arange(RET_HEADS, dtype=np.float64)))
    n = np.arange(chunk, dtype=np.float64)
    diff = n[:, None] - n[None, :]
    mask = np.where(diff >= 0, np.exp(log_g[:, None, None] * np.maximum(diff, 0.0)), 0.0)
    q_dec = np.exp(log_g[:, None] * (n + 1.0))
    k_dec = np.exp(log_g[:, None] * (chunk - 1.0 - n))
    c_dec = np.exp(log_g * chunk)
    return (jnp.asarray(mask, F32),
            jnp.asarray(np.broadcast_to(q_dec[:, :, None], (RET_HEADS, chunk, RET_QK_DIM)), F32),
            jnp.asarray(np.broadcast_to(k_dec[:, :, None], (RET_HEADS, chunk, RET_QK_DIM)), F32),
            jnp.asarray(np.broadcast_to(c_dec[:, None, None], (RET_HEADS, 1, RET_V_DIM)), F32))


def _retention(rqk, rest, batch, seq, chunk):
    consts = _retention_constants(chunk)
    rqk3 = rqk.reshape(batch, seq, 2 * RET_QK_WIDTH)
    rest3 = rest.reshape(batch, seq, rest.shape[1])
    out = pl.pallas_call(
        _retention_kernel,
        grid=(batch, seq // chunk),
        in_specs=[pl.BlockSpec((1, chunk, RET_QK_WIDTH), lambda b, c: (b, c, 0)),
                  pl.BlockSpec((1, chunk, RET_QK_WIDTH), lambda b, c: (b, c, 1)),
                  pl.BlockSpec((1, chunk, RET_V_WIDTH), lambda b, c: (b, c, 0)),
                  pl.BlockSpec((1, chunk, RET_V_WIDTH), lambda b, c: (b, c, 1)),
                  *[_full(a) for a in consts]],
        out_specs=pl.BlockSpec((1, chunk, RET_V_WIDTH), lambda b, c: (b, c, 0)),
        out_shape=jax.ShapeDtypeStruct((batch, seq, RET_V_WIDTH), BF16),
        scratch_shapes=[pltpu.VMEM((RET_HEADS, RET_QK_DIM, RET_V_DIM), F32),
                        pltpu.VMEM((RET_HEADS, chunk, chunk + RET_QK_DIM), BF16),
                        pltpu.VMEM((chunk, RET_V_WIDTH), F32)],
        compiler_params=_params("parallel", "arbitrary"),
        name="retention",
    )(rqk3, rqk3, rest3, rest3, *consts)
    return out.reshape(batch * seq, RET_V_WIDTH)


def _layer_norm(y, g, b):
    mu = jnp.mean(y, axis=-1, keepdims=True)
    cen = y - mu
    var = jnp.mean(cen * cen, axis=-1, keepdims=True)
    return cen * lax.rsqrt(var + LN_EPS) * g + b


def _to_token_order(dst_ref, src_ref):
    dil, sub = src_ref.shape[1], src_ref.shape[2]
    for r in range(dil):
        for c in range(dst_ref.shape[0]):
            dst_ref[c, pl.ds(r, sub, stride=dil), :] = src_ref[0, r, :, c * LANES:(c + 1) * LANES].astype(F32)


def _slab_rows(ref, rows):
    return jnp.concatenate([ref[c, rows, :] for c in range(ref.shape[0])], axis=1)


def _merge_kernel(o0_ref, o1_ref, o2_ref, l0_ref, l1_ref, l2_ref, yb_ref, gates_ref, x_ref,
                  expand_ref, wa_ref, wr_ref, wo_ref, g_ref, b_ref, out_ref,
                  o1_tok, o2_tok, l1_tok, l2_tok):
    for dst, src in ((o1_tok, o1_ref), (o2_tok, o2_ref), (l1_tok, l1_ref), (l2_tok, l2_ref)):
        _to_token_order(dst, src)
    expand = expand_ref[...]
    sub = out_ref.shape[0] // MERGE_SUBTILES
    for t in range(MERGE_SUBTILES):
        rows = slice(t * sub, (t + 1) * sub)
        outs = (o0_ref[0, 0, rows, :].astype(F32), _slab_rows(o1_tok, rows), _slab_rows(o2_tok, rows))
        lses = (l0_ref[0, 0, rows, :], l1_tok[0, rows, :], l2_tok[0, rows, :])
        m = jnp.maximum(jnp.maximum(lses[0], lses[1]), lses[2])
        es = [jnp.exp(l - m) for l in lses]
        inv = 1.0 / (es[0] + es[1] + es[2])
        y_a = jnp.zeros(outs[0].shape, F32)
        for e, o in zip(es, outs):
            wgt = e * inv
            hi = wgt.astype(BF16)
            lo = (wgt - hi.astype(F32)).astype(BF16)
            wide = jnp.dot(jnp.concatenate([hi, lo], axis=1), expand, preferred_element_type=F32)
            y_a = y_a + wide * o
        gates = gates_ref[rows, :].astype(F32)
        branch_a = jnp.dot(y_a.astype(BF16), wa_ref[...], preferred_element_type=F32)
        branch_b = jnp.dot(yb_ref[rows, :], wr_ref[...], preferred_element_type=F32)
        merged = gates[:, :D_MODEL] * branch_a + gates[:, D_MODEL:] * branch_b
        mix = jnp.dot(merged.astype(BF16), wo_ref[...], preferred_element_type=F32)
        out_ref[rows, :] = _layer_norm(ALPHA * x_ref[rows, :] + mix, g_ref[...], b_ref[...])


def _merge(os_, lses, y_b, rest, x, wa, wr, wo, g, b, seq, tm=512):
    n = x.shape[0]
    tiles = seq // tm
    expand = jnp.asarray(
        (np.arange(2 * LANES)[:, None] % LANES == (np.arange(GROUP_WIDTH)[None, :] // HEAD_DIM)).astype(np.float32),
        BF16)
    gate_block = (2 * RET_V_WIDTH) // (2 * D_MODEL)
    row = lambda width: pl.BlockSpec((tm, width), lambda i: (i, 0))

    def strided(a):
        dil, width = a.shape[1], a.shape[3]
        return pl.BlockSpec((1, dil, tm // dil, width), lambda i: (i // tiles, 0, i % tiles, 0))

    return pl.pallas_call(
        _merge_kernel,
        grid=(n // tm,),
        in_specs=[strided(a) for a in os_] + [strided(a) for a in lses]
                 + [row(RET_V_WIDTH), pl.BlockSpec((tm, 2 * D_MODEL), lambda i: (i, gate_block)), row(D_MODEL),
                    _full(expand), _resident(wa), _resident(wr), _resident(wo), _full(g), _full(b)],
        out_specs=row(D_MODEL),
        out_shape=jax.ShapeDtypeStruct((n, D_MODEL), F32),
        scratch_shapes=[pltpu.VMEM((GROUP_WIDTH // LANES, tm, LANES), F32),
                        pltpu.VMEM((GROUP_WIDTH // LANES, tm, LANES), F32),
                        pltpu.VMEM((1, tm, LANES), F32), pltpu.VMEM((1, tm, LANES), F32)],
        compiler_params=_params("parallel"),
        name="merge",
    )(*os_, *lses, y_b, rest, x, expand, wa, wr, wo, g, b)


def _ffn_kernel(x_ref, wg_ref, wu_ref, wd_ref, g_ref, b_ref, out_ref):
    sub = out_ref.shape[0] // FFN_SUBTILES
    for t in range(FFN_SUBTILES):
        rows = slice(t * sub, (t + 1) * sub)
        x = x_ref[rows, :]
        xb = x.astype(BF16)
        gate = jnp.dot(xb, wg_ref[...], preferred_element_type=F32)
        up = jnp.dot(xb, wu_ref[...], preferred_element_type=F32)
        hidden = (gate * _sigmoid(gate) * up).astype(BF16)
        ffn = jnp.dot(hidden, wd_ref[...], preferred_element_type=F32)
        out_ref[rows, :] = _layer_norm(ALPHA * x + ffn, g_ref[...], b_ref[...])


def _ffn(x, wg, wu, wd, g, b, tm=512):
    n = x.shape[0]
    row = pl.BlockSpec((tm, D_MODEL), lambda i: (i, 0))
    return pl.pallas_call(
        _ffn_kernel,
        grid=(n // tm,),
        in_specs=[row, _resident(wg), _resident(wu), _resident(wd), _full(g), _full(b)],
        out_specs=row,
        out_shape=jax.ShapeDtypeStruct((n, D_MODEL), F32),
        compiler_params=_params("parallel"),
        name="ffn",
    )(x, wg, wu, wd, g, b)


def _rope_tables(seq):
    half = RET_QK_DIM // 2
    inv_freq = ROPE_BASE ** (-np.arange(half, dtype=np.float64) / half)
    ang = np.arange(seq, dtype=np.float64)[:, None] * inv_freq[None]
    return jnp.asarray(np.cos(ang), F32), jnp.asarray(np.sin(ang), F32)


def _group_major(a):
    parts = [a[..., (t * N_GROUPS + gi) * GROUP_WIDTH:(t * N_GROUPS + gi + 1) * GROUP_WIDTH]
             for gi in range(N_GROUPS) for t in range(3)]
    return jnp.concatenate(parts, axis=-1)


def kernel(x, rel_bias, w_in, b_in, w_attn_proj, w_ret_proj, w_out, ln1_g, ln1_b,
           w_ffn_gate, w_ffn_up, w_ffn_down, ln2_g, ln2_b):
    batch, seq, d = x.shape
    n = batch * seq
    assert d == D_MODEL and seq % 2048 == 0
    cos, sin = _rope_tables(seq)
    biases = [_attention_bias(rel_bias[:, gi * HEADS_PER_GROUP:(gi + 1) * HEADS_PER_GROUP], dil)
              for gi, (_, dil) in enumerate(ATTN_GROUPS)]
    xf = x.reshape(n, d)
    for l in range(DEPTH):
        w = w_in[l].astype(BF16)
        bias_in = b_in[l].reshape(1, -1)
        *qkvs, rqk, rest = _project(
            xf, _group_major(w[:, :_ATTN_COLS]), w[:, _COL_RQK[0]:_COL_RQK[1]], w[:, _COL_REST[0]:_COL_REST[1]],
            _group_major(bias_in[:, :_ATTN_COLS]), bias_in[:, _COL_RQK[0]:_COL_RQK[1]],
            bias_in[:, _COL_REST[0]:_COL_REST[1]], cos, sin, batch, seq)
        os_, lses = [], []
        for qkv, bias in zip(qkvs, biases):
            o, lse = _attention_group(qkv, bias, q_rows=512)
            os_.append(o)
            lses.append(lse)
        y_b = _retention(rqk, rest, batch, seq, RET_CHUNK)
        xf = _merge(os_, lses, y_b, rest, xf,
                    w_attn_proj[l].astype(BF16), w_ret_proj[l].astype(BF16), w_out[l].astype(BF16),
                    ln1_g[l].reshape(1, d), ln1_b[l].reshape(1, d), seq)
        xf = _ffn(xf, w_ffn_gate[l].astype(BF16), w_ffn_up[l].astype(BF16), w_ffn_down[l].astype(BF16),
                  ln2_g[l].reshape(1, d), ln2_b[l].reshape(1, d))
    return xf.reshape(batch, seq, d)
```

```python
import functools

import numpy as np
import jax
import jax.numpy as jnp
from jax import lax
from jax.experimental import pallas as pl
from jax.experimental.pallas import tpu as pltpu

F32 = jnp.float32
BF16 = jnp.bfloat16

D_MODEL = 1024
DEPTH = 2
HEAD_DIM = 64
ATTN_GROUPS = ((128, 1), (512, 4), (2048, 16))
N_GROUPS = len(ATTN_GROUPS)
HEADS_PER_GROUP = 6
GROUP_WIDTH = HEADS_PER_GROUP * HEAD_DIM
QKV_WIDTH = 3 * GROUP_WIDTH
NUM_BUCKETS = 32
MAX_DISTANCE = 2048
ATTN_BLOCK = 128
LANES = 128
RET_HEADS = 4
RET_QK_DIM = 256
RET_V_DIM = 512
RET_QK_WIDTH = RET_HEADS * RET_QK_DIM
RET_V_WIDTH = RET_HEADS * RET_V_DIM
RET_CHUNK = 256
ROPE_BASE = 10000.0
D_FF = 2816
ALPHA = (2 * DEPTH) ** 0.25
LN_EPS = 1e-5
GN_EPS = 1e-5
MASK_VALUE = -1e30
LOG2E = 1.4426950408889634
LN2 = 0.6931471805599453

_ATTN_COLS = 3 * N_GROUPS * GROUP_WIDTH
_COL_RQK = (_ATTN_COLS, _ATTN_COLS + 2 * RET_QK_WIDTH)
_COL_REST = (_COL_RQK[1], _COL_RQK[1] + 2 * RET_V_WIDTH + 2 * D_MODEL)
_REST_CHUNK = 512

MERGE_SUBTILES = 1
FFN_SUBTILES = 2

VMEM_LIMIT = 56 * 1024 * 1024


def _params(*sem):
    return pltpu.CompilerParams(dimension_semantics=sem, vmem_limit_bytes=VMEM_LIMIT)


def _full(a):
    return pl.BlockSpec(a.shape, lambda *_: (0,) * a.ndim)


def _resident(a):
    return pl.BlockSpec(a.shape, lambda *_: (0,) * a.ndim, pipeline_mode=pl.Buffered(1))


def _sigmoid(x):
    return 0.5 * jnp.tanh(0.5 * x) + 0.5


def _proj_kernel(x_ref, wa_ref, wq_ref, wr_ref, ba_ref, bq_ref, br_ref, cos_ref, sin_ref,
                 o0_ref, o1_ref, o2_ref, rqk_ref, rest_ref, stage_ref):
    tm = x_ref.shape[0]
    xb = x_ref[...].astype(BF16)

    half = RET_QK_DIM // 2
    for t, scale in ((0, 1.0), (1, RET_QK_DIM ** -0.5)):
        c = cos_ref[...] * scale
        s = sin_ref[...] * scale
        for h in range(RET_HEADS):
            lo = t * RET_QK_WIDTH + h * RET_QK_DIM
            acc = (jnp.dot(xb, wq_ref[:, lo:lo + RET_QK_DIM], preferred_element_type=F32)
                   + bq_ref[:, lo:lo + RET_QK_DIM])
            t1 = acc[:, :half]
            t2 = acc[:, half:]
            rqk_ref[:, lo:lo + half] = (t1 * c - t2 * s).astype(rqk_ref.dtype)
            rqk_ref[:, lo + half:lo + RET_QK_DIM] = (t1 * s + t2 * c).astype(rqk_ref.dtype)

    for lo in range(0, wr_ref.shape[1], _REST_CHUNK):
        acc = (jnp.dot(xb, wr_ref[:, lo:lo + _REST_CHUNK], preferred_element_type=F32)
               + br_ref[:, lo:lo + _REST_CHUNK])
        if lo >= 2 * RET_V_WIDTH:
            acc = _sigmoid(acc)
        elif lo >= RET_V_WIDTH:
            acc = acc * _sigmoid(acc)
        rest_ref[:, lo:lo + _REST_CHUNK] = acc.astype(rest_ref.dtype)

    for gi, ((_, dil), o_ref) in enumerate(zip(ATTN_GROUPS, (o0_ref, o1_ref, o2_ref))):
        sub = tm // dil
        cols = slice(gi * QKV_WIDTH, (gi + 1) * QKV_WIDTH)
        acc = jnp.dot(xb, wa_ref[:, cols], preferred_element_type=F32) + ba_ref[:, cols]
        q = acc[:, :GROUP_WIDTH] * (HEAD_DIM ** -0.5 * LOG2E)
        if dil == 1:
            o_ref[0, 0, :, :GROUP_WIDTH] = q.astype(o_ref.dtype)
            o_ref[0, 0, :, GROUP_WIDTH:] = acc[:, GROUP_WIDTH:].astype(o_ref.dtype)
            continue
        for c in range(QKV_WIDTH // LANES):
            lanes = slice(c * LANES, (c + 1) * LANES)
            stage_ref[c] = q[:, lanes] if c < GROUP_WIDTH // LANES else acc[:, lanes]
        for r in range(dil):
            for c in range(QKV_WIDTH // LANES):
                o_ref[0, r, :, c * LANES:(c + 1) * LANES] = (
                    stage_ref[c, pl.ds(r, sub, stride=dil), :].astype(o_ref.dtype))


def _project(x, wa, wq, wr, ba, bq, br, cos, sin, batch, seq, tm=256):
    n, d = x.shape
    tiles = seq // tm
    outs, specs = [], []
    for _, dil in ATTN_GROUPS:
        outs.append(jax.ShapeDtypeStruct((batch, dil, seq // dil, QKV_WIDTH), BF16))
        specs.append(pl.BlockSpec((1, dil, tm // dil, QKV_WIDTH), lambda i: (i // tiles, 0, i % tiles, 0)))
    for w in (wq, wr):
        outs.append(jax.ShapeDtypeStruct((n, w.shape[1]), BF16))
        specs.append(pl.BlockSpec((tm, w.shape[1]), lambda i: (i, 0)))
    rot_spec = pl.BlockSpec((tm, RET_QK_DIM // 2), lambda i: (i % tiles, 0))
    return pl.pallas_call(
        _proj_kernel,
        grid=(n // tm,),
        in_specs=[pl.BlockSpec((tm, d), lambda i: (i, 0)), _resident(wa), _resident(wq), _resident(wr),
                  _resident(ba), _resident(bq), _resident(br), rot_spec, rot_spec],
        out_specs=specs,
        out_shape=outs,
        scratch_shapes=[pltpu.VMEM((QKV_WIDTH // LANES, tm, LANES), F32)],
        compiler_params=_params("parallel"),
        name="proj",
    )(x, wa, wq, wr, ba, bq, br, cos, sin)


def _attn_kernel(q_ref, kp_ref, kc_ref, vp_ref, vc_ref, bias_ref, o_ref, lse_ref, *, sub_blocks):
    first_step = pl.program_id(2) == 0
    w = ATTN_BLOCK
    lane = lax.broadcasted_iota(jnp.int32, (w, LANES), 1)
    low_half = lane < HEAD_DIM
    ones = jnp.ones((2 * w, LANES), BF16)
    for j in range(sub_blocks):
        rows = slice(j * w, (j + 1) * w)
        max_tile = jnp.zeros((w, LANES), F32)
        denom_tile = jnp.ones((w, LANES), F32)
        for pair in range(HEADS_PER_GROUP // 2):
            cols = slice(pair * LANES, (pair + 1) * LANES)
            q = q_ref[0, 0, rows, cols]
            if j == 0:
                k = jnp.concatenate([kp_ref[0, 0, :, cols], kc_ref[0, 0, :w, cols]], axis=0)
                v = jnp.concatenate([vp_ref[0, 0, :, cols], vc_ref[0, 0, :w, cols]], axis=0)
            else:
                k = kc_ref[0, 0, (j - 1) * w:(j + 1) * w, cols]
                v = vc_ref[0, 0, (j - 1) * w:(j + 1) * w, cols]
            v_ones = jnp.concatenate([v, ones], axis=1)
            halves = []
            for sel, hh in ((low_half, 0), (~low_half, 1)):
                h = 2 * pair + hh
                bias = bias_ref[jnp.where(first_step, 0, 1), h] if j == 0 else bias_ref[1, h]
                qh = jnp.where(sel, q, jnp.zeros_like(q))
                s = lax.dot_general(qh, k, (((1,), (1,)), ((), ())), preferred_element_type=F32) + bias
                m = jnp.max(s, axis=-1, keepdims=True)
                p = jnp.exp2(s - m).astype(BF16)
                ov = jnp.dot(p, v_ones, preferred_element_type=F32)
                halves.append(ov)
                max_tile = jnp.where(lane == h, m, max_tile)
                denom_tile = jnp.where(lane == h, ov[:, LANES:], denom_tile)
            numer = jnp.where(low_half, halves[0][:, :LANES], halves[1][:, :LANES])
            denom = jnp.where(low_half, halves[0][:, LANES:], halves[1][:, LANES:])
            o_ref[0, 0, rows, cols] = (numer / denom).astype(o_ref.dtype)
        lse_ref[0, 0, rows, :] = max_tile * LN2 + jnp.log(denom_tile)


def _attention_group(qkv, bias, q_rows):
    batch, dilation, sub_len, _ = qkv.shape
    q_rows = min(q_rows, sub_len)
    sub_blocks = q_rows // ATTN_BLOCK
    steps = sub_len // q_rows

    def cur(unit):
        return pl.BlockSpec((1, 1, q_rows, GROUP_WIDTH), lambda b, r, n: (b, r, n, unit))

    def prev(unit):
        return pl.BlockSpec((1, 1, ATTN_BLOCK, GROUP_WIDTH),
                            lambda b, r, n: (b, r, jnp.maximum(n * sub_blocks - 1, 0), unit))

    return pl.pallas_call(
        functools.partial(_attn_kernel, sub_blocks=sub_blocks),
        grid=(batch, dilation, steps),
        in_specs=[cur(0), prev(1), cur(1), prev(2), cur(2), _full(bias)],
        out_specs=[pl.BlockSpec((1, 1, q_rows, GROUP_WIDTH), lambda b, r, n: (b, r, n, 0)),
                   pl.BlockSpec((1, 1, q_rows, LANES), lambda b, r, n: (b, r, n, 0))],
        out_shape=[jax.ShapeDtypeStruct((batch, dilation, sub_len, GROUP_WIDTH), BF16),
                   jax.ShapeDtypeStruct((batch, dilation, sub_len, LANES), F32)],
        compiler_params=_params("parallel", "parallel", "arbitrary"),
        name=f"attn_d{dilation}",
    )(qkv, qkv, qkv, qkv, qkv, bias)


def _t5_bucket(dist):
    max_exact = NUM_BUCKETS // 2
    large = max_exact + (np.log(np.maximum(dist, max_exact) / max_exact)
                         / np.log(MAX_DISTANCE / max_exact)
                         * (NUM_BUCKETS - max_exact)).astype(np.int32)
    large = np.minimum(large, NUM_BUCKETS - 1)
    return np.where(dist < max_exact, dist, large).astype(np.int32)


def _attention_bias(rel_bias_group, dilation):
    w = ATTN_BLOCK
    buckets = _t5_bucket((w - np.arange(w + 1)) * dilation)
    onehot = jnp.asarray(np.eye(NUM_BUCKETS, dtype=np.float32)[buckets])
    per_c = jnp.einsum('cb,bh->hc', onehot, rel_bias_group.astype(F32),
                       precision=lax.Precision.HIGHEST) * LOG2E
    period = 2 * w + 1
    vec = jnp.concatenate([per_c, jnp.full((HEADS_PER_GROUP, period - (w + 1)), MASK_VALUE, F32)], axis=1)
    normal = jnp.tile(vec, (1, w))[:, :w * 2 * w].reshape(HEADS_PER_GROUP, w, 2 * w)
    first = jnp.where(np.arange(2 * w)[None, None, :] >= w, normal, MASK_VALUE)
    return jnp.stack([first, normal], axis=0)


def _retention_kernel(q_ref, k_ref, v_ref, g_ref, mask_ref, qdec_ref, kdec_ref, cdec_ref, wr_ref, o_ref,
                      state_ref, lhs_ref, acc_ref):
    @pl.when(pl.program_id(1) == 0)
    def _():
        state_ref[...] = jnp.zeros_like(state_ref)

    chunk = q_ref.shape[1]
    qk_cols = [slice(h * RET_QK_DIM, (h + 1) * RET_QK_DIM) for h in range(RET_HEADS)]
    v_cols = [slice(h * RET_V_DIM, (h + 1) * RET_V_DIM) for h in range(RET_HEADS)]
    for h in range(RET_HEADS):
        q = q_ref[0, :, qk_cols[h]]
        k = k_ref[0, :, qk_cols[h]]
        sc = lax.dot_general(q, k, (((1,), (1,)), ((), ())), preferred_element_type=F32) * mask_ref[h]
        lhs_ref[h, :, :chunk] = sc.astype(BF16)
        lhs_ref[h, :, chunk:] = (q.astype(F32) * qdec_ref[h]).astype(BF16)
    for h in range(RET_HEADS):
        rhs = jnp.concatenate([v_ref[0, :, v_cols[h]], state_ref[h].astype(BF16)], axis=0)
        acc_ref[:, v_cols[h]] = jnp.dot(lhs_ref[h], rhs, preferred_element_type=F32)
    for h in range(RET_HEADS):
        kd = (k_ref[0, :, qk_cols[h]].astype(F32) * kdec_ref[h]).astype(BF16)
        state_ref[h] = state_ref[h] * cdec_ref[h] + lax.dot_general(
            kd, v_ref[0, :, v_cols[h]], (((0,), (0,)), ((), ())), preferred_element_type=F32)
    for h in range(RET_HEADS):
        o = acc_ref[:, v_cols[h]]
        mu = jnp.mean(o, axis=-1, keepdims=True)
        cen = o - mu
        var = jnp.mean(cen * cen, axis=-1, keepdims=True)
        y = cen * lax.rsqrt(var + GN_EPS)
        part = jnp.dot(g_ref[0, :, v_cols[h]] * y.astype(BF16), wr_ref[v_cols[h], :], preferred_element_type=F32)
        proj = part if h == 0 else proj + part
    o_ref[0] = proj.astype(o_ref.dtype)


def _retention_constants(chunk):
    log_g = np.log(1.0 - 2.0 ** (-5.0 - np.arange(RET_HEADS, dtype=np.float64)))
    n = np.arange(chunk, dtype=np.float64)
    diff = n[:, None] - n[None, :]
    mask = np.where(diff >= 0, np.exp(log_g[:, None, None] * np.maximum(diff, 0.0)), 0.0)
    q_dec = np.exp(log_g[:, None] * (n + 1.0))
    k_dec = np.exp(log_g[:, None] * (chunk - 1.0 - n))
    c_dec = np.exp(log_g * chunk)
    return (jnp.asarray(mask, F32),
            jnp.asarray(np.broadcast_to(q_dec[:, :, None], (RET_HEADS, chunk, RET_QK_DIM)), F32),
            jnp.asarray(np.broadcast_to(k_dec[:, :, None], (RET_HEADS, chunk, RET_QK_DIM)), F32),
            jnp.asarray(np.broadcast_to(c_dec[:, None, None], (RET_HEADS, 1, RET_V_DIM)), F32))


def _retention(rqk, rest, wr, batch, seq, chunk):
    consts = _retention_constants(chunk)
    rqk3 = rqk.reshape(batch, seq, 2 * RET_QK_WIDTH)
    rest3 = rest.reshape(batch, seq, rest.shape[1])
    out = pl.pallas_call(
        _retention_kernel,
        grid=(batch, seq // chunk),
        in_specs=[pl.BlockSpec((1, chunk, RET_QK_WIDTH), lambda b, c: (b, c, 0)),
                  pl.BlockSpec((1, chunk, RET_QK_WIDTH), lambda b, c: (b, c, 1)),
                  pl.BlockSpec((1, chunk, RET_V_WIDTH), lambda b, c: (b, c, 0)),
                  pl.BlockSpec((1, chunk, RET_V_WIDTH), lambda b, c: (b, c, 1)),
                  *[_full(a) for a in consts], _resident(wr)],
        out_specs=pl.BlockSpec((1, chunk, D_MODEL), lambda b, c: (b, c, 0)),
        out_shape=jax.ShapeDtypeStruct((batch, seq, D_MODEL), BF16),
        scratch_shapes=[pltpu.VMEM((RET_HEADS, RET_QK_DIM, RET_V_DIM), F32),
                        pltpu.VMEM((RET_HEADS, chunk, chunk + RET_QK_DIM), BF16),
                        pltpu.VMEM((chunk, RET_V_WIDTH), F32)],
        compiler_params=_params("parallel", "arbitrary"),
        name="retention",
    )(rqk3, rqk3, rest3, rest3, *consts, wr)
    return out.reshape(batch * seq, D_MODEL)


def _layer_norm(y, g, b):
    mu = jnp.mean(y, axis=-1, keepdims=True)
    cen = y - mu
    var = jnp.mean(cen * cen, axis=-1, keepdims=True)
    return cen * lax.rsqrt(var + LN_EPS) * g + b


def _to_token_order(dst_ref, src_ref):
    dil, sub = src_ref.shape[1], src_ref.shape[2]
    for r in range(dil):
        for c in range(dst_ref.shape[0]):
            dst_ref[c, pl.ds(r, sub, stride=dil), :] = src_ref[0, r, :, c * LANES:(c + 1) * LANES].astype(F32)


def _slab_rows(ref, rows):
    return jnp.concatenate([ref[c, rows, :] for c in range(ref.shape[0])], axis=1)


def _merge_kernel(o0_ref, o1_ref, o2_ref, l0_ref, l1_ref, l2_ref, bb_ref, gates_ref, x_ref,
                  expand_ref, wa_ref, wo_ref, g_ref, b_ref, out_ref,
                  o1_tok, o2_tok, l1_tok, l2_tok):
    for dst, src in ((o1_tok, o1_ref), (o2_tok, o2_ref), (l1_tok, l1_ref), (l2_tok, l2_ref)):
        _to_token_order(dst, src)
    expand = expand_ref[...]
    sub = out_ref.shape[0] // MERGE_SUBTILES
    for t in range(MERGE_SUBTILES):
        rows = slice(t * sub, (t + 1) * sub)
        outs = (o0_ref[0, 0, rows, :].astype(F32), _slab_rows(o1_tok, rows), _slab_rows(o2_tok, rows))
        lses = (l0_ref[0, 0, rows, :], l1_tok[0, rows, :], l2_tok[0, rows, :])
        m = jnp.maximum(jnp.maximum(lses[0], lses[1]), lses[2])
        es = [jnp.exp(l - m) for l in lses]
        inv = 1.0 / (es[0] + es[1] + es[2])
        y_a = jnp.zeros(outs[0].shape, F32)
        for e, o in zip(es, outs):
            wgt = e * inv
            hi = wgt.astype(BF16)
            lo = (wgt - hi.astype(F32)).astype(BF16)
            wide = jnp.dot(jnp.concatenate([hi, lo], axis=1), expand, preferred_element_type=F32)
            y_a = y_a + wide * o
        gates = gates_ref[rows, :].astype(F32)
        branch_a = jnp.dot(y_a.astype(BF16), wa_ref[...], preferred_element_type=F32)
        merged = gates[:, :D_MODEL] * branch_a + gates[:, D_MODEL:] * bb_ref[rows, :].astype(F32)
        mix = jnp.dot(merged.astype(BF16), wo_ref[...], preferred_element_type=F32)
        out_ref[rows, :] = _layer_norm(ALPHA * x_ref[rows, :] + mix, g_ref[...], b_ref[...])


def _merge(os_, lses, branch_b, rest, x, wa, wo, g, b, seq, tm=512):
    n = x.shape[0]
    tiles = seq // tm
    expand = jnp.asarray(
        (np.arange(2 * LANES)[:, None] % LANES == (np.arange(GROUP_WIDTH)[None, :] // HEAD_DIM)).astype(np.float32),
        BF16)
    gate_block = (2 * RET_V_WIDTH) // (2 * D_MODEL)
    row = lambda width: pl.BlockSpec((tm, width), lambda i: (i, 0))

    def strided(a):
        dil, width = a.shape[1], a.shape[3]
        return pl.BlockSpec((1, dil, tm // dil, width), lambda i: (i // tiles, 0, i % tiles, 0))

    return pl.pallas_call(
        _merge_kernel,
        grid=(n // tm,),
        in_specs=[strided(a) for a in os_] + [strided(a) for a in lses]
                 + [row(D_MODEL), pl.BlockSpec((tm, 2 * D_MODEL), lambda i: (i, gate_block)), row(D_MODEL),
                    _full(expand), _resident(wa), _resident(wo), _full(g), _full(b)],
        out_specs=row(D_MODEL),
        out_shape=jax.ShapeDtypeStruct((n, D_MODEL), F32),
        scratch_shapes=[pltpu.VMEM((GROUP_WIDTH // LANES, tm, LANES), F32),
                        pltpu.VMEM((GROUP_WIDTH // LANES, tm, LANES), F32),
                        pltpu.VMEM((1, tm, LANES), F32), pltpu.VMEM((1, tm, LANES), F32)],
        compiler_params=_params("parallel"),
        name="merge",
    )(*os_, *lses, branch_b, rest, x, expand, wa, wo, g, b)


def _ffn_kernel(x_ref, wg_ref, wu_ref, wd_ref, g_ref, b_ref, out_ref):
    sub = out_ref.shape[0] // FFN_SUBTILES
    for t in range(FFN_SUBTILES):
        rows = slice(t * sub, (t + 1) * sub)
        x = x_ref[rows, :]
        xb = x.astype(BF16)
        gate = jnp.dot(xb, wg_ref[...], preferred_element_type=F32)
        up = jnp.dot(xb, wu_ref[...], preferred_element_type=F32)
        hidden = (gate * _sigmoid(gate) * up).astype(BF16)
        ffn = jnp.dot(hidden, wd_ref[...], preferred_element_type=F32)
        out_ref[rows, :] = _layer_norm(ALPHA * x + ffn, g_ref[...], b_ref[...])


def _ffn(x, wg, wu, wd, g, b, tm=512):
    n = x.shape[0]
    row = pl.BlockSpec((tm, D_MODEL), lambda i: (i, 0))
    return pl.pallas_call(
        _ffn_kernel,
        grid=(n // tm,),
        in_specs=[row, _resident(wg), _resident(wu), _resident(wd), _full(g), _full(b)],
        out_specs=row,
        out_shape=jax.ShapeDtypeStruct((n, D_MODEL), F32),
        compiler_params=_params("parallel"),
        name="ffn",
    )(x, wg, wu, wd, g, b)


def _rope_tables(seq):
    half = RET_QK_DIM // 2
    inv_freq = ROPE_BASE ** (-np.arange(half, dtype=np.float64) / half)
    ang = np.arange(seq, dtype=np.float64)[:, None] * inv_freq[None]
    return jnp.asarray(np.cos(ang), F32), jnp.asarray(np.sin(ang), F32)


def _group_major(a):
    parts = [a[..., (t * N_GROUPS + gi) * GROUP_WIDTH:(t * N_GROUPS + gi + 1) * GROUP_WIDTH]
             for gi in range(N_GROUPS) for t in range(3)]
    return jnp.concatenate(parts, axis=-1)


def kernel(x, rel_bias, w_in, b_in, w_attn_proj, w_ret_proj, w_out, ln1_g, ln1_b,
           w_ffn_gate, w_ffn_up, w_ffn_down, ln2_g, ln2_b):
    batch, seq, d = x.shape
    n = batch * seq
    assert d == D_MODEL and seq % 2048 == 0
    cos, sin = _rope_tables(seq)
    biases = [_attention_bias(rel_bias[:, gi * HEADS_PER_GROUP:(gi + 1) * HEADS_PER_GROUP], dil)
              for gi, (_, dil) in enumerate(ATTN_GROUPS)]
    xf = x.reshape(n, d)
    for l in range(DEPTH):
        w = w_in[l].astype(BF16)
        bias_in = b_in[l].reshape(1, -1)
        *qkvs, rqk, rest = _project(
            xf, _group_major(w[:, :_ATTN_COLS]), w[:, _COL_RQK[0]:_COL_RQK[1]], w[:, _COL_REST[0]:_COL_REST[1]],
            _group_major(bias_in[:, :_ATTN_COLS]), bias_in[:, _COL_RQK[0]:_COL_RQK[1]],
            bias_in[:, _COL_REST[0]:_COL_REST[1]], cos, sin, batch, seq)
        os_, lses = [], []
        for qkv, bias in zip(qkvs, biases):
            o, lse = _attention_group(qkv, bias, q_rows=512)
            os_.append(o)
            lses.append(lse)
        branch_b = _retention(rqk, rest, w_ret_proj[l].astype(BF16), batch, seq, RET_CHUNK)
        xf = _merge(os_, lses, branch_b, rest, xf,
                    w_attn_proj[l].astype(BF16), w_out[l].astype(BF16),
                    ln1_g[l].reshape(1, d), ln1_b[l].reshape(1, d), seq)
        xf = _ffn(xf, w_ffn_gate[l].astype(BF16), w_ffn_up[l].astype(BF16), w_ffn_down[l].astype(BF16),
                  ln2_g[l].reshape(1, d), ln2_b[l].reshape(1, d))
    return xf.reshape(batch, seq, d)
```

```python
import functools

import numpy as np
import jax
import jax.numpy as jnp
from jax import lax
from jax.experimental import pallas as pl
from jax.experimental.pallas import tpu as pltpu

F32 = jnp.float32
BF16 = jnp.bfloat16

D_MODEL = 1024
DEPTH = 2
HEAD_DIM = 64
ATTN_GROUPS = ((128, 1), (512, 4), (2048, 16))
N_GROUPS = len(ATTN_GROUPS)
HEADS_PER_GROUP = 6
GROUP_WIDTH = HEADS_PER_GROUP * HEAD_DIM
QKV_WIDTH = 3 * GROUP_WIDTH
NUM_BUCKETS = 32
MAX_DISTANCE = 2048
ATTN_BLOCK = 128
LANES = 128
RET_HEADS = 4
RET_QK_DIM = 256
RET_V_DIM = 512
RET_QK_WIDTH = RET_HEADS * RET_QK_DIM
RET_V_WIDTH = RET_HEADS * RET_V_DIM
RET_CHUNK = 256
ROPE_BASE = 10000.0
D_FF = 2816
ALPHA = (2 * DEPTH) ** 0.25
LN_EPS = 1e-5
GN_EPS = 1e-5
MASK_VALUE = -1e30
LOG2E = 1.4426950408889634
LN2 = 0.6931471805599453

_ATTN_COLS = 3 * N_GROUPS * GROUP_WIDTH
_COL_RQK = (_ATTN_COLS, _ATTN_COLS + 2 * RET_QK_WIDTH)
_COL_REST = (_COL_RQK[1], _COL_RQK[1] + 2 * RET_V_WIDTH + 2 * D_MODEL)
_REST_CHUNK = 512

MERGE_SUBTILES = 2
FFN_SUBTILES = 2

VMEM_LIMIT = 56 * 1024 * 1024


def _params(*sem):
    return pltpu.CompilerParams(dimension_semantics=sem, vmem_limit_bytes=VMEM_LIMIT)


def _full(a):
    return pl.BlockSpec(a.shape, lambda *_: (0,) * a.ndim)


def _resident(a):
    return pl.BlockSpec(a.shape, lambda *_: (0,) * a.ndim, pipeline_mode=pl.Buffered(1))


def _sigmoid(x):
    return 0.5 * jnp.tanh(0.5 * x) + 0.5


def _proj_kernel(x_ref, wa_ref, wq_ref, wr_ref, ba_ref, bq_ref, br_ref, cos_ref, sin_ref,
                 o0_ref, o1_ref, o2_ref, rqk_ref, rest_ref, stage_ref):
    tm = x_ref.shape[0]
    xb = x_ref[...].astype(BF16)

    half = RET_QK_DIM // 2
    for t, scale in ((0, 1.0), (1, RET_QK_DIM ** -0.5)):
        c = cos_ref[...] * scale
        s = sin_ref[...] * scale
        for h in range(RET_HEADS):
            lo = t * RET_QK_WIDTH + h * RET_QK_DIM
            acc = (jnp.dot(xb, wq_ref[:, lo:lo + RET_QK_DIM], preferred_element_type=F32)
                   + bq_ref[:, lo:lo + RET_QK_DIM])
            t1 = acc[:, :half]
            t2 = acc[:, half:]
            rqk_ref[:, lo:lo + half] = (t1 * c - t2 * s).astype(rqk_ref.dtype)
            rqk_ref[:, lo + half:lo + RET_QK_DIM] = (t1 * s + t2 * c).astype(rqk_ref.dtype)

    for lo in range(0, wr_ref.shape[1], _REST_CHUNK):
        acc = (jnp.dot(xb, wr_ref[:, lo:lo + _REST_CHUNK], preferred_element_type=F32)
               + br_ref[:, lo:lo + _REST_CHUNK])
        if lo >= 2 * RET_V_WIDTH:
            acc = _sigmoid(acc)
        elif lo >= RET_V_WIDTH:
            acc = acc * _sigmoid(acc)
        rest_ref[:, lo:lo + _REST_CHUNK] = acc.astype(rest_ref.dtype)

    pair_cols = slice(0, 2 * QKV_WIDTH)
    acc01 = jnp.dot(xb, wa_ref[:, pair_cols], preferred_element_type=F32) + ba_ref[:, pair_cols]
    last_cols = slice(2 * QKV_WIDTH, 3 * QKV_WIDTH)
    acc2 = jnp.dot(xb, wa_ref[:, last_cols], preferred_element_type=F32) + ba_ref[:, last_cols]
    accs = (acc01[:, :QKV_WIDTH], acc01[:, QKV_WIDTH:], acc2)
    for acc, (_, dil), o_ref in zip(accs, ATTN_GROUPS, (o0_ref, o1_ref, o2_ref)):
        sub = tm // dil
        q = acc[:, :GROUP_WIDTH] * (HEAD_DIM ** -0.5 * LOG2E)
        if dil == 1:
            o_ref[0, 0, :, :GROUP_WIDTH] = q.astype(o_ref.dtype)
            o_ref[0, 0, :, GROUP_WIDTH:] = acc[:, GROUP_WIDTH:].astype(o_ref.dtype)
            continue
        for c in range(QKV_WIDTH // LANES):
            lanes = slice(c * LANES, (c + 1) * LANES)
            stage_ref[c] = q[:, lanes] if c < GROUP_WIDTH // LANES else acc[:, lanes]
        for r in range(dil):
            for c in range(QKV_WIDTH // LANES):
                o_ref[0, r, :, c * LANES:(c + 1) * LANES] = (
                    stage_ref[c, pl.ds(r, sub, stride=dil), :].astype(o_ref.dtype))


def _project(x, wa, wq, wr, ba, bq, br, cos, sin, batch, seq, tm=256):
    n, d = x.shape
    tiles = seq // tm
    outs, specs = [], []
    for _, dil in ATTN_GROUPS:
        outs.append(jax.ShapeDtypeStruct((batch, dil, seq // dil, QKV_WIDTH), BF16))
        specs.append(pl.BlockSpec((1, dil, tm // dil, QKV_WIDTH), lambda i: (i // tiles, 0, i % tiles, 0)))
    for w in (wq, wr):
        outs.append(jax.ShapeDtypeStruct((n, w.shape[1]), BF16))
        specs.append(pl.BlockSpec((tm, w.shape[1]), lambda i: (i, 0)))
    rot_spec = pl.BlockSpec((tm, RET_QK_DIM // 2), lambda i: (i % tiles, 0))
    return pl.pallas_call(
        _proj_kernel,
        grid=(n // tm,),
        in_specs=[pl.BlockSpec((tm, d), lambda i: (i, 0)), _resident(wa), _resident(wq), _resident(wr),
                  _resident(ba), _resident(bq), _resident(br), rot_spec, rot_spec],
        out_specs=specs,
        out_shape=outs,
        scratch_shapes=[pltpu.VMEM((QKV_WIDTH // LANES, tm, LANES), F32)],
        compiler_params=_params("parallel"),
        name="proj",
    )(x, wa, wq, wr, ba, bq, br, cos, sin)


def _attn_kernel(q_ref, kp_ref, kc_ref, vp_ref, vc_ref, bias_ref, o_ref, lse_ref, *, sub_blocks):
    first_step = pl.program_id(2) == 0
    w = ATTN_BLOCK
    lane = lax.broadcasted_iota(jnp.int32, (w, LANES), 1)
    low_half = lane < HEAD_DIM
    ones = jnp.ones((2 * w, LANES), BF16)
    for r in range(q_ref.shape[1]):
        for j in range(sub_blocks):
            rows = slice(j * w, (j + 1) * w)
            max_tile = jnp.zeros((w, LANES), F32)
            denom_tile = jnp.ones((w, LANES), F32)
            for pair in range(HEADS_PER_GROUP // 2):
                cols = slice(pair * LANES, (pair + 1) * LANES)
                q = q_ref[0, r, rows, cols]
                if j == 0:
                    k = jnp.concatenate([kp_ref[0, r, :, cols], kc_ref[0, r, :w, cols]], axis=0)
                    v = jnp.concatenate([vp_ref[0, r, :, cols], vc_ref[0, r, :w, cols]], axis=0)
                else:
                    k = kc_ref[0, r, (j - 1) * w:(j + 1) * w, cols]
                    v = vc_ref[0, r, (j - 1) * w:(j + 1) * w, cols]
                v_ones = jnp.concatenate([v, ones], axis=1)
                halves = []
                for sel, hh in ((low_half, 0), (~low_half, 1)):
                    h = 2 * pair + hh
                    bias = bias_ref[jnp.where(first_step, 0, 1), h] if j == 0 else bias_ref[1, h]
                    qh = jnp.where(sel, q, jnp.zeros_like(q))
                    s = lax.dot_general(qh, k, (((1,), (1,)), ((), ())), preferred_element_type=F32) + bias
                    m = jnp.max(s, axis=-1, keepdims=True)
                    p = jnp.exp2(s - m).astype(BF16)
                    ov = jnp.dot(p, v_ones, preferred_element_type=F32)
                    halves.append(ov)
                    max_tile = jnp.where(lane == h, m, max_tile)
                    denom_tile = jnp.where(lane == h, ov[:, LANES:], denom_tile)
                numer = jnp.where(low_half, halves[0][:, :LANES], halves[1][:, :LANES])
                denom = jnp.where(low_half, halves[0][:, LANES:], halves[1][:, LANES:])
                o_ref[0, r, rows, cols] = (numer / denom).astype(o_ref.dtype)
            lse_ref[0, r, rows, :] = max_tile * LN2 + jnp.log(denom_tile)


def _attention_group(qkv, bias, step_rows):
    batch, dilation, sub_len, _ = qkv.shape
    q_rows = min(step_rows, sub_len)
    res = min(step_rows // q_rows, dilation)
    sub_blocks = q_rows // ATTN_BLOCK
    steps = sub_len // q_rows

    def cur(unit):
        return pl.BlockSpec((1, res, q_rows, GROUP_WIDTH), lambda b, r, n: (b, r, n, unit))

    def prev(unit):
        return pl.BlockSpec((1, res, ATTN_BLOCK, GROUP_WIDTH),
                            lambda b, r, n: (b, r, jnp.maximum(n * sub_blocks - 1, 0), unit))

    return pl.pallas_call(
        functools.partial(_attn_kernel, sub_blocks=sub_blocks),
        grid=(batch, dilation // res, steps),
        in_specs=[cur(0), prev(1), cur(1), prev(2), cur(2), _full(bias)],
        out_specs=[pl.BlockSpec((1, res, q_rows, GROUP_WIDTH), lambda b, r, n: (b, r, n, 0)),
                   pl.BlockSpec((1, res, q_rows, LANES), lambda b, r, n: (b, r, n, 0))],
        out_shape=[jax.ShapeDtypeStruct((batch, dilation, sub_len, GROUP_WIDTH), BF16),
                   jax.ShapeDtypeStruct((batch, dilation, sub_len, LANES), F32)],
        compiler_params=_params("parallel", "parallel", "arbitrary"),
        name=f"attn_d{dilation}",
    )(qkv, qkv, qkv, qkv, qkv, bias)


def _t5_bucket(dist):
    max_exact = NUM_BUCKETS // 2
    large = max_exact + (np.log(np.maximum(dist, max_exact) / max_exact)
                         / np.log(MAX_DISTANCE / max_exact)
                         * (NUM_BUCKETS - max_exact)).astype(np.int32)
    large = np.minimum(large, NUM_BUCKETS - 1)
    return np.where(dist < max_exact, dist, large).astype(np.int32)


def _attention_bias(rel_bias_group, dilation):
    w = ATTN_BLOCK
    buckets = _t5_bucket((w - np.arange(w + 1)) * dilation)
    onehot = jnp.asarray(np.eye(NUM_BUCKETS, dtype=np.float32)[buckets])
    per_c = jnp.einsum('cb,bh->hc', onehot, rel_bias_group.astype(F32),
                       precision=lax.Precision.HIGHEST) * LOG2E
    period = 2 * w + 1
    vec = jnp.concatenate([per_c, jnp.full((HEADS_PER_GROUP, period - (w + 1)), MASK_VALUE, F32)], axis=1)
    normal = jnp.tile(vec, (1, w))[:, :w * 2 * w].reshape(HEADS_PER_GROUP, w, 2 * w)
    first = jnp.where(np.arange(2 * w)[None, None, :] >= w, normal, MASK_VALUE)
    return jnp.stack([first, normal], axis=0)


def _retention_kernel(q_ref, k_ref, v_ref, g_ref, mask_ref, qdec_ref, kdec_ref, cdec_ref, wr_ref, o_ref,
                      state_ref, lhs_ref, acc_ref):
    @pl.when(pl.program_id(1) == 0)
    def _():
        state_ref[...] = jnp.zeros_like(state_ref)

    chunk = q_ref.shape[1]
    qk_cols = [slice(h * RET_QK_DIM, (h + 1) * RET_QK_DIM) for h in range(RET_HEADS)]
    v_cols = [slice(h * RET_V_DIM, (h + 1) * RET_V_DIM) for h in range(RET_HEADS)]
    for h in range(RET_HEADS):
        q = q_ref[0, :, qk_cols[h]]
        k = k_ref[0, :, qk_cols[h]]
        sc = lax.dot_general(q, k, (((1,), (1,)), ((), ())), preferred_element_type=F32) * mask_ref[h]
        lhs_ref[h, :, :chunk] = sc.astype(BF16)
        lhs_ref[h, :, chunk:] = (q.astype(F32) * qdec_ref[h]).astype(BF16)
    for h in range(RET_HEADS):
        rhs = jnp.concatenate([v_ref[0, :, v_cols[h]], state_ref[h].astype(BF16)], axis=0)
        acc_ref[:, v_cols[h]] = jnp.dot(lhs_ref[h], rhs, preferred_element_type=F32)
    for h in range(RET_HEADS):
        kd = (k_ref[0, :, qk_cols[h]].astype(F32) * kdec_ref[h]).astype(BF16)
        state_ref[h] = state_ref[h] * cdec_ref[h] + lax.dot_general(
            kd, v_ref[0, :, v_cols[h]], (((0,), (0,)), ((), ())), preferred_element_type=F32)
    for h in range(RET_HEADS):
        o = acc_ref[:, v_cols[h]]
        mu = jnp.mean(o, axis=-1, keepdims=True)
        cen = o - mu
        var = jnp.mean(cen * cen, axis=-1, keepdims=True)
        y = cen * lax.rsqrt(var + GN_EPS)
        part = jnp.dot(g_ref[0, :, v_cols[h]] * y.astype(BF16), wr_ref[v_cols[h], :], preferred_element_type=F32)
        proj = part if h == 0 else proj + part
    o_ref[0] = proj.astype(o_ref.dtype)


def _retention_constants(chunk):
    log_g = np.log(1.0 - 2.0 ** (-5.0 - np.arange(RET_HEADS, dtype=np.float64)))
    n = np.arange(chunk, dtype=np.float64)
    diff = n[:, None] - n[None, :]
    mask = np.where(diff >= 0, np.exp(log_g[:, None, None] * np.maximum(diff, 0.0)), 0.0)
    q_dec = np.exp(log_g[:, None] * (n + 1.0))
    k_dec = np.exp(log_g[:, None] * (chunk - 1.0 - n))
    c_dec = np.exp(log_g * chunk)
    return (jnp.asarray(mask, F32),
            jnp.asarray(np.broadcast_to(q_dec[:, :, None], (RET_HEADS, chunk, RET_QK_DIM)), F32),
            jnp.asarray(np.broadcast_to(k_dec[:, :, None], (RET_HEADS, chunk, RET_QK_DIM)), F32),
            jnp.asarray(np.broadcast_to(c_dec[:, None, None], (RET_HEADS, 1, RET_V_DIM)), F32))


def _retention(rqk, rest, wr, batch, seq, chunk):
    consts = _retention_constants(chunk)
    rqk3 = rqk.reshape(batch, seq, 2 * RET_QK_WIDTH)
    rest3 = rest.reshape(batch, seq, rest.shape[1])
    out = pl.pallas_call(
        _retention_kernel,
        grid=(batch, seq // chunk),
        in_specs=[pl.BlockSpec((1, chunk, RET_QK_WIDTH), lambda b, c: (b, c, 0)),
                  pl.BlockSpec((1, chunk, RET_QK_WIDTH), lambda b, c: (b, c, 1)),
                  pl.BlockSpec((1, chunk, RET_V_WIDTH), lambda b, c: (b, c, 0)),
                  pl.BlockSpec((1, chunk, RET_V_WIDTH), lambda b, c: (b, c, 1)),
                  *[_full(a) for a in consts], _resident(wr)],
        out_specs=pl.BlockSpec((1, chunk, D_MODEL), lambda b, c: (b, c, 0)),
        out_shape=jax.ShapeDtypeStruct((batch, seq, D_MODEL), BF16),
        scratch_shapes=[pltpu.VMEM((RET_HEADS, RET_QK_DIM, RET_V_DIM), F32),
                        pltpu.VMEM((RET_HEADS, chunk, chunk + RET_QK_DIM), BF16),
                        pltpu.VMEM((chunk, RET_V_WIDTH), F32)],
        compiler_params=_params("parallel", "arbitrary"),
        name="retention",
    )(rqk3, rqk3, rest3, rest3, *consts, wr)
    return out.reshape(batch * seq, D_MODEL)


def _layer_norm(y, g, b):
    mu = jnp.mean(y, axis=-1, keepdims=True)
    cen = y - mu
    var = jnp.mean(cen * cen, axis=-1, keepdims=True)
    return cen * lax.rsqrt(var + LN_EPS) * g + b


def _to_token_order(dst_ref, src_ref):
    dil, sub = src_ref.shape[1], src_ref.shape[2]
    for r in range(dil):
        for c in range(dst_ref.shape[0]):
            dst_ref[c, pl.ds(r, sub, stride=dil), :] = src_ref[0, r, :, c * LANES:(c + 1) * LANES].astype(F32)


def _slab_rows(ref, rows):
    return jnp.concatenate([ref[c, rows, :] for c in range(ref.shape[0])], axis=1)


def _merge_kernel(o0_ref, o1_ref, o2_ref, l0_ref, l1_ref, l2_ref, bb_ref, gates_ref, x_ref,
                  expand_ref, wa_ref, wo_ref, g_ref, b_ref, out_ref,
                  o1_tok, o2_tok, l1_tok, l2_tok):
    for dst, src in ((o1_tok, o1_ref), (o2_tok, o2_ref), (l1_tok, l1_ref), (l2_tok, l2_ref)):
        _to_token_order(dst, src)
    expand = expand_ref[...]
    sub = out_ref.shape[0] // MERGE_SUBTILES
    for t in range(MERGE_SUBTILES):
        rows = slice(t * sub, (t + 1) * sub)
        outs = (o0_ref[0, 0, rows, :].astype(F32), _slab_rows(o1_tok, rows), _slab_rows(o2_tok, rows))
        lses = (l0_ref[0, 0, rows, :], l1_tok[0, rows, :], l2_tok[0, rows, :])
        m = jnp.maximum(jnp.maximum(lses[0], lses[1]), lses[2])
        es = [jnp.exp(l - m) for l in lses]
        inv = 1.0 / (es[0] + es[1] + es[2])
        y_a = jnp.zeros(outs[0].shape, F32)
        for e, o in zip(es, outs):
            wgt = e * inv
            hi = wgt.astype(BF16)
            lo = (wgt - hi.astype(F32)).astype(BF16)
            wide = jnp.dot(jnp.concatenate([hi, lo], axis=1), expand, preferred_element_type=F32)
            y_a = y_a + wide * o
        gates = gates_ref[rows, :].astype(F32)
        branch_a = jnp.dot(y_a.astype(BF16), wa_ref[...], preferred_element_type=F32)
        merged = gates[:, :D_MODEL] * branch_a + gates[:, D_MODEL:] * bb_ref[rows, :].astype(F32)
        mix = jnp.dot(merged.astype(BF16), wo_ref[...], preferred_element_type=F32)
        out_ref[rows, :] = _layer_norm(ALPHA * x_ref[rows, :] + mix, g_ref[...], b_ref[...])


def _merge(os_, lses, branch_b, rest, x, wa, wo, g, b, seq, tm=512):
    n = x.shape[0]
    tiles = seq // tm
    expand = jnp.asarray(
        (np.arange(2 * LANES)[:, None] % LANES == (np.arange(GROUP_WIDTH)[None, :] // HEAD_DIM)).astype(np.float32),
        BF16)
    gate_block = (2 * RET_V_WIDTH) // (2 * D_MODEL)
    row = lambda width: pl.BlockSpec((tm, width), lambda i: (i, 0))

    def strided(a):
        dil, width = a.shape[1], a.shape[3]
        return pl.BlockSpec((1, dil, tm // dil, width), lambda i: (i // tiles, 0, i % tiles, 0))

    return pl.pallas_call(
        _merge_kernel,
        grid=(n // tm,),
        in_specs=[strided(a) for a in os_] + [strided(a) for a in lses]
                 + [row(D_MODEL), pl.BlockSpec((tm, 2 * D_MODEL), lambda i: (i, gate_block)), row(D_MODEL),
                    _full(expand), _resident(wa), _resident(wo), _full(g), _full(b)],
        out_specs=row(D_MODEL),
        out_shape=jax.ShapeDtypeStruct((n, D_MODEL), F32),
        scratch_shapes=[pltpu.VMEM((GROUP_WIDTH // LANES, tm, LANES), F32),
                        pltpu.VMEM((GROUP_WIDTH // LANES, tm, LANES), F32),
                        pltpu.VMEM((1, tm, LANES), F32), pltpu.VMEM((1, tm, LANES), F32)],
        compiler_params=_params("parallel"),
        name="merge",
    )(*os_, *lses, branch_b, rest, x, expand, wa, wo, g, b)


def _ffn_kernel(x_ref, wg_ref, wu_ref, wd_ref, g_ref, b_ref, out_ref):
    sub = out_ref.shape[0] // FFN_SUBTILES
    for t in range(FFN_SUBTILES):
        rows = slice(t * sub, (t + 1) * sub)
        x = x_ref[rows, :]
        xb = x.astype(BF16)
        gate = jnp.dot(xb, wg_ref[...], preferred_element_type=F32)
        up = jnp.dot(xb, wu_ref[...], preferred_element_type=F32)
        hidden = (gate * _sigmoid(gate) * up).astype(BF16)
        ffn = jnp.dot(hidden, wd_ref[...], preferred_element_type=F32)
        out_ref[rows, :] = _layer_norm(ALPHA * x + ffn, g_ref[...], b_ref[...])


def _ffn(x, wg, wu, wd, g, b, tm=512):
    n = x.shape[0]
    row = pl.BlockSpec((tm, D_MODEL), lambda i: (i, 0))
    return pl.pallas_call(
        _ffn_kernel,
        grid=(n // tm,),
        in_specs=[row, _resident(wg), _resident(wu), _resident(wd), _full(g), _full(b)],
        out_specs=row,
        out_shape=jax.ShapeDtypeStruct((n, D_MODEL), F32),
        compiler_params=_params("parallel"),
        name="ffn",
    )(x, wg, wu, wd, g, b)


def _rope_tables(seq):
    half = RET_QK_DIM // 2
    inv_freq = ROPE_BASE ** (-np.arange(half, dtype=np.float64) / half)
    ang = np.arange(seq, dtype=np.float64)[:, None] * inv_freq[None]
    return jnp.asarray(np.cos(ang), F32), jnp.asarray(np.sin(ang), F32)


def _group_major(a):
    parts = [a[..., (t * N_GROUPS + gi) * GROUP_WIDTH:(t * N_GROUPS + gi + 1) * GROUP_WIDTH]
             for gi in range(N_GROUPS) for t in range(3)]
    return jnp.concatenate(parts, axis=-1)


def kernel(x, rel_bias, w_in, b_in, w_attn_proj, w_ret_proj, w_out, ln1_g, ln1_b,
           w_ffn_gate, w_ffn_up, w_ffn_down, ln2_g, ln2_b):
    batch, seq, d = x.shape
    n = batch * seq
    assert d == D_MODEL and seq % 2048 == 0
    cos, sin = _rope_tables(seq)
    biases = [_attention_bias(rel_bias[:, gi * HEADS_PER_GROUP:(gi + 1) * HEADS_PER_GROUP], dil)
              for gi, (_, dil) in enumerate(ATTN_GROUPS)]
    xf = x.reshape(n, d)
    for l in range(DEPTH):
        w = w_in[l]
        bias_in = b_in[l].reshape(1, -1)
        *qkvs, rqk, rest = _project(
            xf, _group_major(w[:, :_ATTN_COLS]).astype(BF16), w[:, _COL_RQK[0]:_COL_RQK[1]].astype(BF16),
            w[:, _COL_REST[0]:_COL_REST[1]].astype(BF16),
            _group_major(bias_in[:, :_ATTN_COLS]), bias_in[:, _COL_RQK[0]:_COL_RQK[1]],
            bias_in[:, _COL_REST[0]:_COL_REST[1]], cos, sin, batch, seq)
        os_, lses = [], []
        for qkv, bias in zip(qkvs, biases):
            o, lse = _attention_group(qkv, bias, step_rows=1024)
            os_.append(o)
            lses.append(lse)
        branch_b = _retention(rqk, rest, w_ret_proj[l].astype(BF16), batch, seq, RET_CHUNK)
        xf = _merge(os_, lses, branch_b, rest, xf,
                    w_attn_proj[l].astype(BF16), w_out[l].astype(BF16),
                    ln1_g[l].reshape(1, d), ln1_b[l].reshape(1, d), seq)
        xf = _ffn(xf, w_ffn_gate[l].astype(BF16), w_ffn_up[l].astype(BF16), w_ffn_down[l].astype(BF16),
                  ln2_g[l].reshape(1, d), ln2_b[l].reshape(1, d))
    return xf.reshape(batch, seq, d)
```

```python
import functools

import numpy as np
import jax
import jax.numpy as jnp
from jax import lax
from jax.experimental import pallas as pl
from jax.experimental.pallas import tpu as pltpu

F32 = jnp.float32
BF16 = jnp.bfloat16

D_MODEL = 1024
DEPTH = 2
HEAD_DIM = 64
ATTN_GROUPS = ((128, 1), (512, 4), (2048, 16))
N_GROUPS = len(ATTN_GROUPS)
HEADS_PER_GROUP = 6
GROUP_WIDTH = HEADS_PER_GROUP * HEAD_DIM
QKV_WIDTH = 3 * GROUP_WIDTH
NUM_BUCKETS = 32
MAX_DISTANCE = 2048
ATTN_BLOCK = 128
LANES = 128
RET_HEADS = 4
RET_QK_DIM = 256
RET_V_DIM = 512
RET_QK_WIDTH = RET_HEADS * RET_QK_DIM
RET_V_WIDTH = RET_HEADS * RET_V_DIM
RET_CHUNK = 256
ROPE_BASE = 10000.0
D_FF = 2816
ALPHA = (2 * DEPTH) ** 0.25
LN_EPS = 1e-5
GN_EPS = 1e-5
MASK_VALUE = -1e30
LOG2E = 1.4426950408889634
LN2 = 0.6931471805599453

_ATTN_COLS = 3 * N_GROUPS * GROUP_WIDTH
_COL_RQK = (_ATTN_COLS, _ATTN_COLS + 2 * RET_QK_WIDTH)
_COL_REST = (_COL_RQK[1], _COL_RQK[1] + 2 * RET_V_WIDTH + 2 * D_MODEL)
_REST_CHUNK = 512

MERGE_SUBTILES = 4
FFN_SUBTILES = 4

VMEM_LIMIT = 60 * 1024 * 1024


def _params(*sem):
    return pltpu.CompilerParams(dimension_semantics=sem, vmem_limit_bytes=VMEM_LIMIT)


def _full(a):
    return pl.BlockSpec(a.shape, lambda *_: (0,) * a.ndim)


def _resident(a):
    return pl.BlockSpec(a.shape, lambda *_: (0,) * a.ndim, pipeline_mode=pl.Buffered(1))


def _sigmoid(x):
    return 0.5 * jnp.tanh(0.5 * x) + 0.5


def _proj_kernel(x_ref, wa_ref, wq_ref, wr_ref, ba_ref, bq_ref, br_ref, cos_ref, sin_ref,
                 o0_ref, o1_ref, o2_ref, rqk_ref, rest_ref, stage_ref):
    tm = x_ref.shape[0]
    xb = x_ref[...].astype(BF16)

    half = RET_QK_DIM // 2
    for t, scale in ((0, 1.0), (1, RET_QK_DIM ** -0.5)):
        c = cos_ref[...] * scale
        s = sin_ref[...] * scale
        for h in range(RET_HEADS):
            lo = t * RET_QK_WIDTH + h * RET_QK_DIM
            acc = (jnp.dot(xb, wq_ref[:, lo:lo + RET_QK_DIM], preferred_element_type=F32)
                   + bq_ref[:, lo:lo + RET_QK_DIM])
            t1 = acc[:, :half]
            t2 = acc[:, half:]
            rqk_ref[:, lo:lo + half] = (t1 * c - t2 * s).astype(rqk_ref.dtype)
            rqk_ref[:, lo + half:lo + RET_QK_DIM] = (t1 * s + t2 * c).astype(rqk_ref.dtype)

    for lo in range(0, wr_ref.shape[1], _REST_CHUNK):
        acc = (jnp.dot(xb, wr_ref[:, lo:lo + _REST_CHUNK], preferred_element_type=F32)
               + br_ref[:, lo:lo + _REST_CHUNK])
        if lo >= 2 * RET_V_WIDTH:
            acc = _sigmoid(acc)
        elif lo >= RET_V_WIDTH:
            acc = acc * _sigmoid(acc)
        rest_ref[:, lo:lo + _REST_CHUNK] = acc.astype(rest_ref.dtype)

    pair_cols = slice(0, 2 * QKV_WIDTH)
    acc01 = jnp.dot(xb, wa_ref[:, pair_cols], preferred_element_type=F32) + ba_ref[:, pair_cols]
    last_cols = slice(2 * QKV_WIDTH, 3 * QKV_WIDTH)
    acc2 = jnp.dot(xb, wa_ref[:, last_cols], preferred_element_type=F32) + ba_ref[:, last_cols]
    accs = (acc01[:, :QKV_WIDTH], acc01[:, QKV_WIDTH:], acc2)
    for acc, (_, dil), o_ref in zip(accs, ATTN_GROUPS, (o0_ref, o1_ref, o2_ref)):
        sub = tm // dil
        q = acc[:, :GROUP_WIDTH] * (HEAD_DIM ** -0.5 * LOG2E)
        if dil == 1:
            o_ref[0, 0, :, :GROUP_WIDTH] = q.astype(o_ref.dtype)
            o_ref[0, 0, :, GROUP_WIDTH:] = acc[:, GROUP_WIDTH:].astype(o_ref.dtype)
            continue
        for c in range(QKV_WIDTH // LANES):
            lanes = slice(c * LANES, (c + 1) * LANES)
            stage_ref[c] = q[:, lanes] if c < GROUP_WIDTH // LANES else acc[:, lanes]
        for r in range(dil):
            for c in range(QKV_WIDTH // LANES):
                o_ref[0, r, :, c * LANES:(c + 1) * LANES] = (
                    stage_ref[c, pl.ds(r, sub, stride=dil), :].astype(o_ref.dtype))


def _project(x, wa, wq, wr, ba, bq, br, cos, sin, batch, seq, tm=512):
    n, d = x.shape
    tiles = seq // tm
    outs, specs = [], []
    for _, dil in ATTN_GROUPS:
        outs.append(jax.ShapeDtypeStruct((batch, dil, seq // dil, QKV_WIDTH), BF16))
        specs.append(pl.BlockSpec((1, dil, tm // dil, QKV_WIDTH), lambda i: (i // tiles, 0, i % tiles, 0)))
    for w in (wq, wr):
        outs.append(jax.ShapeDtypeStruct((n, w.shape[1]), BF16))
        specs.append(pl.BlockSpec((tm, w.shape[1]), lambda i: (i, 0)))
    rot_spec = pl.BlockSpec((tm, RET_QK_DIM // 2), lambda i: (i % tiles, 0))
    return pl.pallas_call(
        _proj_kernel,
        grid=(n // tm,),
        in_specs=[pl.BlockSpec((tm, d), lambda i: (i, 0)), _resident(wa), _resident(wq), _resident(wr),
                  _resident(ba), _resident(bq), _resident(br), rot_spec, rot_spec],
        out_specs=specs,
        out_shape=outs,
        scratch_shapes=[pltpu.VMEM((QKV_WIDTH // LANES, tm, LANES), F32)],
        compiler_params=_params("parallel"),
        name="proj",
    )(x, wa, wq, wr, ba, bq, br, cos, sin)


def _attn_kernel(q_ref, kp_ref, kc_ref, vp_ref, vc_ref, bias_ref, o_ref, lse_ref, *, sub_blocks):
    first_step = pl.program_id(2) == 0
    w = ATTN_BLOCK
    lane = lax.broadcasted_iota(jnp.int32, (w, LANES), 1)
    low_half = lane < HEAD_DIM
    ones = jnp.ones((2 * w, LANES), BF16)
    for r in range(q_ref.shape[1]):
        for j in range(sub_blocks):
            rows = slice(j * w, (j + 1) * w)
            max_tile = jnp.zeros((w, LANES), F32)
            denom_tile = jnp.ones((w, LANES), F32)
            for pair in range(HEADS_PER_GROUP // 2):
                cols = slice(pair * LANES, (pair + 1) * LANES)
                q = q_ref[0, r, rows, cols]
                if j == 0:
                    k = jnp.concatenate([kp_ref[0, r, :, cols], kc_ref[0, r, :w, cols]], axis=0)
                    v = jnp.concatenate([vp_ref[0, r, :, cols], vc_ref[0, r, :w, cols]], axis=0)
                else:
                    k = kc_ref[0, r, (j - 1) * w:(j + 1) * w, cols]
                    v = vc_ref[0, r, (j - 1) * w:(j + 1) * w, cols]
                v_ones = jnp.concatenate([v, ones], axis=1)
                halves = []
                for sel, hh in ((low_half, 0), (~low_half, 1)):
                    h = 2 * pair + hh
                    bias = bias_ref[jnp.where(first_step, 0, 1), h] if j == 0 else bias_ref[1, h]
                    qh = jnp.where(sel, q, jnp.zeros_like(q))
                    s = lax.dot_general(qh, k, (((1,), (1,)), ((), ())), preferred_element_type=F32) + bias
                    m = jnp.max(s, axis=-1, keepdims=True)
                    p = jnp.exp2(s - m).astype(BF16)
                    ov = jnp.dot(p, v_ones, preferred_element_type=F32)
                    halves.append(ov)
                    max_tile = jnp.where(lane == h, m, max_tile)
                    denom_tile = jnp.where(lane == h, ov[:, LANES:], denom_tile)
                numer = jnp.where(low_half, halves[0][:, :LANES], halves[1][:, :LANES])
                denom = jnp.where(low_half, halves[0][:, LANES:], halves[1][:, LANES:])
                o_ref[0, r, rows, cols] = (numer / denom).astype(o_ref.dtype)
            lse_ref[0, r, rows, :] = max_tile * LN2 + jnp.log(denom_tile)


def _attention_group(qkv, bias, step_rows):
    batch, dilation, sub_len, _ = qkv.shape
    q_rows = min(step_rows, sub_len)
    res = min(step_rows // q_rows, dilation)
    sub_blocks = q_rows // ATTN_BLOCK
    steps = sub_len // q_rows

    def cur(unit):
        return pl.BlockSpec((1, res, q_rows, GROUP_WIDTH), lambda b, r, n: (b, r, n, unit))

    def prev(unit):
        return pl.BlockSpec((1, res, ATTN_BLOCK, GROUP_WIDTH),
                            lambda b, r, n: (b, r, jnp.maximum(n * sub_blocks - 1, 0), unit))

    return pl.pallas_call(
        functools.partial(_attn_kernel, sub_blocks=sub_blocks),
        grid=(batch, dilation // res, steps),
        in_specs=[cur(0), prev(1), cur(1), prev(2), cur(2), _full(bias)],
        out_specs=[pl.BlockSpec((1, res, q_rows, GROUP_WIDTH), lambda b, r, n: (b, r, n, 0)),
                   pl.BlockSpec((1, res, q_rows, LANES), lambda b, r, n: (b, r, n, 0))],
        out_shape=[jax.ShapeDtypeStruct((batch, dilation, sub_len, GROUP_WIDTH), BF16),
                   jax.ShapeDtypeStruct((batch, dilation, sub_len, LANES), F32)],
        compiler_params=_params("parallel", "parallel", "arbitrary"),
        name=f"attn_d{dilation}",
    )(qkv, qkv, qkv, qkv, qkv, bias)


def _t5_bucket(dist):
    max_exact = NUM_BUCKETS // 2
    large = max_exact + (np.log(np.maximum(dist, max_exact) / max_exact)
                         / np.log(MAX_DISTANCE / max_exact)
                         * (NUM_BUCKETS - max_exact)).astype(np.int32)
    large = np.minimum(large, NUM_BUCKETS - 1)
    return np.where(dist < max_exact, dist, large).astype(np.int32)


def _attention_bias(rel_bias_group, dilation):
    w = ATTN_BLOCK
    buckets = _t5_bucket((w - np.arange(w + 1)) * dilation)
    onehot = jnp.asarray(np.eye(NUM_BUCKETS, dtype=np.float32)[buckets])
    per_c = jnp.einsum('cb,bh->hc', onehot, rel_bias_group.astype(F32),
                       precision=lax.Precision.HIGHEST) * LOG2E
    period = 2 * w + 1
    vec = jnp.concatenate([per_c, jnp.full((HEADS_PER_GROUP, period - (w + 1)), MASK_VALUE, F32)], axis=1)
    normal = jnp.tile(vec, (1, w))[:, :w * 2 * w].reshape(HEADS_PER_GROUP, w, 2 * w)
    first = jnp.where(np.arange(2 * w)[None, None, :] >= w, normal, MASK_VALUE)
    return jnp.stack([first, normal], axis=0)


def _retention_kernel(q_ref, k_ref, v_ref, g_ref, mask_ref, qdec_ref, kdec_ref, cdec_ref, wr_ref, o_ref,
                      state_ref, lhs_ref, acc_ref):
    @pl.when(pl.program_id(1) == 0)
    def _():
        state_ref[...] = jnp.zeros_like(state_ref)

    chunk = q_ref.shape[1]
    qk_cols = [slice(h * RET_QK_DIM, (h + 1) * RET_QK_DIM) for h in range(RET_HEADS)]
    v_cols = [slice(h * RET_V_DIM, (h + 1) * RET_V_DIM) for h in range(RET_HEADS)]
    for h in range(RET_HEADS):
        q = q_ref[0, :, qk_cols[h]]
        k = k_ref[0, :, qk_cols[h]]
        sc = lax.dot_general(q, k, (((1,), (1,)), ((), ())), preferred_element_type=F32) * mask_ref[h]
        lhs_ref[h, :, :chunk] = sc.astype(BF16)
        lhs_ref[h, :, chunk:] = (q.astype(F32) * qdec_ref[h]).astype(BF16)
    for h in range(RET_HEADS):
        rhs = jnp.concatenate([v_ref[0, :, v_cols[h]], state_ref[h].astype(BF16)], axis=0)
        acc_ref[:, v_cols[h]] = jnp.dot(lhs_ref[h], rhs, preferred_element_type=F32)
    for h in range(RET_HEADS):
        kd = (k_ref[0, :, qk_cols[h]].astype(F32) * kdec_ref[h]).astype(BF16)
        state_ref[h] = state_ref[h] * cdec_ref[h] + lax.dot_general(
            kd, v_ref[0, :, v_cols[h]], (((0,), (0,)), ((), ())), preferred_element_type=F32)
    for h in range(RET_HEADS):
        o = acc_ref[:, v_cols[h]]
        mu = jnp.mean(o, axis=-1, keepdims=True)
        cen = o - mu
        var = jnp.mean(cen * cen, axis=-1, keepdims=True)
        y = cen * lax.rsqrt(var + GN_EPS)
        part = jnp.dot(g_ref[0, :, v_cols[h]] * y.astype(BF16), wr_ref[v_cols[h], :], preferred_element_type=F32)
        proj = part if h == 0 else proj + part
    o_ref[0] = proj.astype(o_ref.dtype)


def _retention_constants(chunk):
    log_g = np.log(1.0 - 2.0 ** (-5.0 - np.arange(RET_HEADS, dtype=np.float64)))
    n = np.arange(chunk, dtype=np.float64)
    diff = n[:, None] - n[None, :]
    mask = np.where(diff >= 0, np.exp(log_g[:, None, None] * np.maximum(diff, 0.0)), 0.0)
    q_dec = np.exp(log_g[:, None] * (n + 1.0))
    k_dec = np.exp(log_g[:, None] * (chunk - 1.0 - n))
    c_dec = np.exp(log_g * chunk)
    return (jnp.asarray(mask, F32),
            jnp.asarray(np.broadcast_to(q_dec[:, :, None], (RET_HEADS, chunk, RET_QK_DIM)), F32),
            jnp.asarray(np.broadcast_to(k_dec[:, :, None], (RET_HEADS, chunk, RET_QK_DIM)), F32),
            jnp.asarray(np.broadcast_to(c_dec[:, None, None], (RET_HEADS, 1, RET_V_DIM)), F32))


def _retention(rqk, rest, wr, batch, seq, chunk):
    consts = _retention_constants(chunk)
    rqk3 = rqk.reshape(batch, seq, 2 * RET_QK_WIDTH)
    rest3 = rest.reshape(batch, seq, rest.shape[1])
    out = pl.pallas_call(
        _retention_kernel,
        grid=(batch, seq // chunk),
        in_specs=[pl.BlockSpec((1, chunk, RET_QK_WIDTH), lambda b, c: (b, c, 0)),
                  pl.BlockSpec((1, chunk, RET_QK_WIDTH), lambda b, c: (b, c, 1)),
                  pl.BlockSpec((1, chunk, RET_V_WIDTH), lambda b, c: (b, c, 0)),
                  pl.BlockSpec((1, chunk, RET_V_WIDTH), lambda b, c: (b, c, 1)),
                  *[_full(a) for a in consts], _resident(wr)],
        out_specs=pl.BlockSpec((1, chunk, D_MODEL), lambda b, c: (b, c, 0)),
        out_shape=jax.ShapeDtypeStruct((batch, seq, D_MODEL), BF16),
        scratch_shapes=[pltpu.VMEM((RET_HEADS, RET_QK_DIM, RET_V_DIM), F32),
                        pltpu.VMEM((RET_HEADS, chunk, chunk + RET_QK_DIM), BF16),
                        pltpu.VMEM((chunk, RET_V_WIDTH), F32)],
        compiler_params=_params("parallel", "arbitrary"),
        name="retention",
    )(rqk3, rqk3, rest3, rest3, *consts, wr)
    return out.reshape(batch * seq, D_MODEL)


def _layer_norm(y, g, b):
    mu = jnp.mean(y, axis=-1, keepdims=True)
    cen = y - mu
    var = jnp.mean(cen * cen, axis=-1, keepdims=True)
    return cen * lax.rsqrt(var + LN_EPS) * g + b


def _to_token_order(dst_ref, src_ref):
    dil, sub = src_ref.shape[1], src_ref.shape[2]
    for r in range(dil):
        for c in range(dst_ref.shape[0]):
            dst_ref[c, pl.ds(r, sub, stride=dil), :] = src_ref[0, r, :, c * LANES:(c + 1) * LANES].astype(F32)


def _slab_rows(ref, rows):
    return jnp.concatenate([ref[c, rows, :] for c in range(ref.shape[0])], axis=1)


def _merge_kernel(o0_ref, o1_ref, o2_ref, l0_ref, l1_ref, l2_ref, bb_ref, gates_ref, x_ref,
                  expand_ref, wa_ref, wo_ref, g_ref, b_ref, out_ref,
                  o1_tok, o2_tok, l1_tok, l2_tok):
    for dst, src in ((o1_tok, o1_ref), (o2_tok, o2_ref), (l1_tok, l1_ref), (l2_tok, l2_ref)):
        _to_token_order(dst, src)
    expand = expand_ref[...]
    sub = out_ref.shape[0] // MERGE_SUBTILES
    for t in range(MERGE_SUBTILES):
        rows = slice(t * sub, (t + 1) * sub)
        outs = (o0_ref[0, 0, rows, :].astype(F32), _slab_rows(o1_tok, rows), _slab_rows(o2_tok, rows))
        lses = (l0_ref[0, 0, rows, :], l1_tok[0, rows, :], l2_tok[0, rows, :])
        m = jnp.maximum(jnp.maximum(lses[0], lses[1]), lses[2])
        es = [jnp.exp(l - m) for l in lses]
        inv = 1.0 / (es[0] + es[1] + es[2])
        y_a = jnp.zeros(outs[0].shape, F32)
        for e, o in zip(es, outs):
            wgt = e * inv
            hi = wgt.astype(BF16)
            lo = (wgt - hi.astype(F32)).astype(BF16)
            wide = jnp.dot(jnp.concatenate([hi, lo], axis=1), expand, preferred_element_type=F32)
            y_a = y_a + wide * o
        gates = gates_ref[rows, :].astype(F32)
        branch_a = jnp.dot(y_a.astype(BF16), wa_ref[...], preferred_element_type=F32)
        merged = gates[:, :D_MODEL] * branch_a + gates[:, D_MODEL:] * bb_ref[rows, :].astype(F32)
        mix = jnp.dot(merged.astype(BF16), wo_ref[...], preferred_element_type=F32)
        out_ref[rows, :] = _layer_norm(ALPHA * x_ref[rows, :] + mix, g_ref[...], b_ref[...])


def _merge(os_, lses, branch_b, rest, x, wa, wo, g, b, seq, tm=1024):
    n = x.shape[0]
    tiles = seq // tm
    expand = jnp.asarray(
        (np.arange(2 * LANES)[:, None] % LANES == (np.arange(GROUP_WIDTH)[None, :] // HEAD_DIM)).astype(np.float32),
        BF16)
    gate_block = (2 * RET_V_WIDTH) // (2 * D_MODEL)
    row = lambda width: pl.BlockSpec((tm, width), lambda i: (i, 0))

    def strided(a):
        dil, width = a.shape[1], a.shape[3]
        return pl.BlockSpec((1, dil, tm // dil, width), lambda i: (i // tiles, 0, i % tiles, 0))

    return pl.pallas_call(
        _merge_kernel,
        grid=(n // tm,),
        in_specs=[strided(a) for a in os_] + [strided(a) for a in lses]
                 + [row(D_MODEL), pl.BlockSpec((tm, 2 * D_MODEL), lambda i: (i, gate_block)), row(D_MODEL),
                    _full(expand), _resident(wa), _resident(wo), _full(g), _full(b)],
        out_specs=row(D_MODEL),
        out_shape=jax.ShapeDtypeStruct((n, D_MODEL), F32),
        scratch_shapes=[pltpu.VMEM((GROUP_WIDTH // LANES, tm, LANES), F32),
                        pltpu.VMEM((GROUP_WIDTH // LANES, tm, LANES), F32),
                        pltpu.VMEM((1, tm, LANES), F32), pltpu.VMEM((1, tm, LANES), F32)],
        compiler_params=_params("parallel"),
        name="merge",
    )(*os_, *lses, branch_b, rest, x, expand, wa, wo, g, b)


def _ffn_kernel(x_ref, wg_ref, wu_ref, wd_ref, g_ref, b_ref, out_ref):
    sub = out_ref.shape[0] // FFN_SUBTILES
    for t in range(FFN_SUBTILES):
        rows = slice(t * sub, (t + 1) * sub)
        x = x_ref[rows, :]
        xb = x.astype(BF16)
        gate = jnp.dot(xb, wg_ref[...], preferred_element_type=F32)
        up = jnp.dot(xb, wu_ref[...], preferred_element_type=F32)
        hidden = (gate * _sigmoid(gate) * up).astype(BF16)
        ffn = jnp.dot(hidden, wd_ref[...], preferred_element_type=F32)
        out_ref[rows, :] = _layer_norm(ALPHA * x + ffn, g_ref[...], b_ref[...])


def _ffn(x, wg, wu, wd, g, b, tm=1024):
    n = x.shape[0]
    row = pl.BlockSpec((tm, D_MODEL), lambda i: (i, 0))
    return pl.pallas_call(
        _ffn_kernel,
        grid=(n // tm,),
        in_specs=[row, _resident(wg), _resident(wu), _resident(wd), _full(g), _full(b)],
        out_specs=row,
        out_shape=jax.ShapeDtypeStruct((n, D_MODEL), F32),
        compiler_params=_params("parallel"),
        name="ffn",
    )(x, wg, wu, wd, g, b)


def _rope_tables(seq):
    half = RET_QK_DIM // 2
    inv_freq = ROPE_BASE ** (-np.arange(half, dtype=np.float64) / half)
    ang = np.arange(seq, dtype=np.float64)[:, None] * inv_freq[None]
    return jnp.asarray(np.cos(ang), F32), jnp.asarray(np.sin(ang), F32)


def _group_major(a):
    parts = [a[..., (t * N_GROUPS + gi) * GROUP_WIDTH:(t * N_GROUPS + gi + 1) * GROUP_WIDTH]
             for gi in range(N_GROUPS) for t in range(3)]
    return jnp.concatenate(parts, axis=-1)


def kernel(x, rel_bias, w_in, b_in, w_attn_proj, w_ret_proj, w_out, ln1_g, ln1_b,
           w_ffn_gate, w_ffn_up, w_ffn_down, ln2_g, ln2_b):
    batch, seq, d = x.shape
    n = batch * seq
    assert d == D_MODEL and seq % 2048 == 0
    cos, sin = _rope_tables(seq)
    biases = [_attention_bias(rel_bias[:, gi * HEADS_PER_GROUP:(gi + 1) * HEADS_PER_GROUP], dil)
              for gi, (_, dil) in enumerate(ATTN_GROUPS)]
    xf = x.reshape(n, d)
    for l in range(DEPTH):
        w = w_in[l]
        bias_in = b_in[l].reshape(1, -1)
        *qkvs, rqk, rest = _project(
            xf, _group_major(w[:, :_ATTN_COLS]).astype(BF16), w[:, _COL_RQK[0]:_COL_RQK[1]].astype(BF16),
            w[:, _COL_REST[0]:_COL_REST[1]].astype(BF16),
            _group_major(bias_in[:, :_ATTN_COLS]), bias_in[:, _COL_RQK[0]:_COL_RQK[1]],
            bias_in[:, _COL_REST[0]:_COL_REST[1]], cos, sin, batch, seq)
        os_, lses = [], []
        for qkv, bias in zip(qkvs, biases):
            o, lse = _attention_group(qkv, bias, step_rows=2048)
            os_.append(o)
            lses.append(lse)
        branch_b = _retention(rqk, rest, w_ret_proj[l].astype(BF16), batch, seq, RET_CHUNK)
        xf = _merge(os_, lses, branch_b, rest, xf,
                    w_attn_proj[l].astype(BF16), w_out[l].astype(BF16),
                    ln1_g[l].reshape(1, d), ln1_b[l].reshape(1, d), seq)
        xf = _ffn(xf, w_ffn_gate[l].astype(BF16), w_ffn_up[l].astype(BF16), w_ffn_down[l].astype(BF16),
                  ln2_g[l].reshape(1, d), ln2_b[l].reshape(1, d))
    return xf.reshape(batch, seq, d)
```

```python
import functools

import numpy as np
import jax
import jax.numpy as jnp
from jax import lax
from jax.experimental import pallas as pl
from jax.experimental.pallas import tpu as pltpu

F32 = jnp.float32
BF16 = jnp.bfloat16

D_MODEL = 1024
DEPTH = 2
HEAD_DIM = 64
ATTN_GROUPS = ((128, 1), (512, 4), (2048, 16))
N_GROUPS = len(ATTN_GROUPS)
HEADS_PER_GROUP = 6
GROUP_WIDTH = HEADS_PER_GROUP * HEAD_DIM
QKV_WIDTH = 3 * GROUP_WIDTH
NUM_BUCKETS = 32
MAX_DISTANCE = 2048
ATTN_BLOCK = 128
LANES = 128
RET_HEADS = 4
RET_QK_DIM = 256
RET_V_DIM = 512
RET_QK_WIDTH = RET_HEADS * RET_QK_DIM
RET_V_WIDTH = RET_HEADS * RET_V_DIM
RET_CHUNK = 256
RET_BATCH_ROWS = 2
ROPE_BASE = 10000.0
D_FF = 2816
ALPHA = (2 * DEPTH) ** 0.25
LN_EPS = 1e-5
GN_EPS = 1e-5
MASK_VALUE = -1e30
LOG2E = 1.4426950408889634
LN2 = 0.6931471805599453

_ATTN_COLS = 3 * N_GROUPS * GROUP_WIDTH
_COL_RQK = (_ATTN_COLS, _ATTN_COLS + 2 * RET_QK_WIDTH)
_COL_REST = (_COL_RQK[1], _COL_RQK[1] + 2 * RET_V_WIDTH + 2 * D_MODEL)
_REST_CHUNK = 512

PROJ_SUBTILES = 2
MERGE_SUBTILES = 4
FFN_SUBTILES = 4

VMEM_LIMIT = 60 * 1024 * 1024


def _params(*sem):
    return pltpu.CompilerParams(dimension_semantics=sem, vmem_limit_bytes=VMEM_LIMIT)


def _full(a):
    return pl.BlockSpec(a.shape, lambda *_: (0,) * a.ndim)


def _resident(a):
    return pl.BlockSpec(a.shape, lambda *_: (0,) * a.ndim, pipeline_mode=pl.Buffered(1))


def _sigmoid(x):
    return 0.5 * jnp.tanh(0.5 * x) + 0.5


def _proj_kernel(x_ref, wa_ref, wq_ref, wr_ref, ba_ref, bq_ref, br_ref, cos_ref, sin_ref,
                 o0_ref, o1_ref, o2_ref, rqk_ref, rest_ref, stage_ref):
    sub_rows = x_ref.shape[0] // PROJ_SUBTILES
    for t in range(PROJ_SUBTILES):
        rows = slice(t * sub_rows, (t + 1) * sub_rows)
        xb = x_ref[rows, :].astype(BF16)

        half = RET_QK_DIM // 2
        for qk, scale in ((0, 1.0), (1, RET_QK_DIM ** -0.5)):
            c = cos_ref[rows, :] * scale
            s = sin_ref[rows, :] * scale
            for h in range(RET_HEADS):
                lo = qk * RET_QK_WIDTH + h * RET_QK_DIM
                acc = (jnp.dot(xb, wq_ref[:, lo:lo + RET_QK_DIM], preferred_element_type=F32)
                       + bq_ref[:, lo:lo + RET_QK_DIM])
                t1 = acc[:, :half]
                t2 = acc[:, half:]
                rqk_ref[rows, lo:lo + half] = (t1 * c - t2 * s).astype(rqk_ref.dtype)
                rqk_ref[rows, lo + half:lo + RET_QK_DIM] = (t1 * s + t2 * c).astype(rqk_ref.dtype)

        for lo in range(0, wr_ref.shape[1], _REST_CHUNK):
            acc = (jnp.dot(xb, wr_ref[:, lo:lo + _REST_CHUNK], preferred_element_type=F32)
                   + br_ref[:, lo:lo + _REST_CHUNK])
            if lo >= 2 * RET_V_WIDTH:
                acc = _sigmoid(acc)
            elif lo >= RET_V_WIDTH:
                acc = acc * _sigmoid(acc)
            rest_ref[rows, lo:lo + _REST_CHUNK] = acc.astype(rest_ref.dtype)

        pair_cols = slice(0, 2 * QKV_WIDTH)
        acc01 = jnp.dot(xb, wa_ref[:, pair_cols], preferred_element_type=F32) + ba_ref[:, pair_cols]
        last_cols = slice(2 * QKV_WIDTH, 3 * QKV_WIDTH)
        acc2 = jnp.dot(xb, wa_ref[:, last_cols], preferred_element_type=F32) + ba_ref[:, last_cols]
        accs = (acc01[:, :QKV_WIDTH], acc01[:, QKV_WIDTH:], acc2)
        for acc, (_, dil), o_ref in zip(accs, ATTN_GROUPS, (o0_ref, o1_ref, o2_ref)):
            sub = sub_rows // dil
            dst = slice(t * sub, (t + 1) * sub)
            q = acc[:, :GROUP_WIDTH] * (HEAD_DIM ** -0.5 * LOG2E)
            if dil == 1:
                o_ref[0, 0, dst, :GROUP_WIDTH] = q.astype(o_ref.dtype)
                o_ref[0, 0, dst, GROUP_WIDTH:] = acc[:, GROUP_WIDTH:].astype(o_ref.dtype)
                continue
            for c in range(QKV_WIDTH // LANES):
                lanes = slice(c * LANES, (c + 1) * LANES)
                stage_ref[c] = q[:, lanes] if c < GROUP_WIDTH // LANES else acc[:, lanes]
            for r in range(dil):
                for c in range(QKV_WIDTH // LANES):
                    o_ref[0, r, dst, c * LANES:(c + 1) * LANES] = (
                        stage_ref[c, pl.ds(r, sub, stride=dil), :].astype(o_ref.dtype))


def _project(x, wa, wq, wr, ba, bq, br, cos, sin, batch, seq, tm=512):
    n, d = x.shape
    tiles = seq // tm
    outs, specs = [], []
    for _, dil in ATTN_GROUPS:
        outs.append(jax.ShapeDtypeStruct((batch, dil, seq // dil, QKV_WIDTH), BF16))
        specs.append(pl.BlockSpec((1, dil, tm // dil, QKV_WIDTH), lambda i: (i // tiles, 0, i % tiles, 0)))
    for w in (wq, wr):
        outs.append(jax.ShapeDtypeStruct((n, w.shape[1]), BF16))
        specs.append(pl.BlockSpec((tm, w.shape[1]), lambda i: (i, 0)))
    rot_spec = pl.BlockSpec((tm, RET_QK_DIM // 2), lambda i: (i % tiles, 0))
    return pl.pallas_call(
        _proj_kernel,
        grid=(n // tm,),
        in_specs=[pl.BlockSpec((tm, d), lambda i: (i, 0)), _resident(wa), _resident(wq), _resident(wr),
                  _resident(ba), _resident(bq), _resident(br), rot_spec, rot_spec],
        out_specs=specs,
        out_shape=outs,
        scratch_shapes=[pltpu.VMEM((QKV_WIDTH // LANES, tm // PROJ_SUBTILES, LANES), F32)],
        compiler_params=_params("parallel"),
        name="proj",
    )(x, wa, wq, wr, ba, bq, br, cos, sin)


def _attn_kernel(q_ref, kp_ref, kc_ref, vp_ref, vc_ref, bias_ref, o_ref, lse_ref, *, sub_blocks):
    first_step = pl.program_id(2) == 0
    w = ATTN_BLOCK
    lane = lax.broadcasted_iota(jnp.int32, (w, LANES), 1)
    low_half = lane < HEAD_DIM
    ones = jnp.ones((2 * w, LANES), BF16)
    for r in range(q_ref.shape[1]):
        for j in range(sub_blocks):
            rows = slice(j * w, (j + 1) * w)
            max_tile = jnp.zeros((w, LANES), F32)
            denom_tile = jnp.ones((w, LANES), F32)
            for pair in range(HEADS_PER_GROUP // 2):
                cols = slice(pair * LANES, (pair + 1) * LANES)
                q = q_ref[0, r, rows, cols]
                if j == 0:
                    k = jnp.concatenate([kp_ref[0, r, :, cols], kc_ref[0, r, :w, cols]], axis=0)
                    v = jnp.concatenate([vp_ref[0, r, :, cols], vc_ref[0, r, :w, cols]], axis=0)
                else:
                    k = kc_ref[0, r, (j - 1) * w:(j + 1) * w, cols]
                    v = vc_ref[0, r, (j - 1) * w:(j + 1) * w, cols]
                v_ones = jnp.concatenate([v, ones], axis=1)
                halves = []
                for sel, hh in ((low_half, 0), (~low_half, 1)):
                    h = 2 * pair + hh
                    bias = bias_ref[jnp.where(first_step, 0, 1), h] if j == 0 else bias_ref[1, h]
                    qh = jnp.where(sel, q, jnp.zeros_like(q))
                    s = lax.dot_general(qh, k, (((1,), (1,)), ((), ())), preferred_element_type=F32) + bias
                    m = jnp.max(s, axis=-1, keepdims=True)
                    p = jnp.exp2(s - m).astype(BF16)
                    ov = jnp.dot(p, v_ones, preferred_element_type=F32)
                    halves.append(ov)
                    max_tile = jnp.where(lane == h, m, max_tile)
                    denom_tile = jnp.where(lane == h, ov[:, LANES:], denom_tile)
                numer = jnp.where(low_half, halves[0][:, :LANES], halves[1][:, :LANES])
                denom = jnp.where(low_half, halves[0][:, LANES:], halves[1][:, LANES:])
                o_ref[0, r, rows, cols] = (numer / denom).astype(o_ref.dtype)
            lse_ref[0, r, rows, :] = max_tile * LN2 + jnp.log(denom_tile)


def _attention_group(qkv, bias, step_rows):
    batch, dilation, sub_len, _ = qkv.shape
    q_rows = min(step_rows, sub_len)
    res = min(step_rows // q_rows, dilation)
    sub_blocks = q_rows // ATTN_BLOCK
    steps = sub_len // q_rows

    def cur(unit):
        return pl.BlockSpec((1, res, q_rows, GROUP_WIDTH), lambda b, r, n: (b, r, n, unit))

    def prev(unit):
        return pl.BlockSpec((1, res, ATTN_BLOCK, GROUP_WIDTH),
                            lambda b, r, n: (b, r, jnp.maximum(n * sub_blocks - 1, 0), unit))

    return pl.pallas_call(
        functools.partial(_attn_kernel, sub_blocks=sub_blocks),
        grid=(batch, dilation // res, steps),
        in_specs=[cur(0), prev(1), cur(1), prev(2), cur(2), _full(bias)],
        out_specs=[pl.BlockSpec((1, res, q_rows, GROUP_WIDTH), lambda b, r, n: (b, r, n, 0)),
                   pl.BlockSpec((1, res, q_rows, LANES), lambda b, r, n: (b, r, n, 0))],
        out_shape=[jax.ShapeDtypeStruct((batch, dilation, sub_len, GROUP_WIDTH), BF16),
                   jax.ShapeDtypeStruct((batch, dilation, sub_len, LANES), F32)],
        compiler_params=_params("parallel", "parallel", "arbitrary"),
        name=f"attn_d{dilation}",
    )(qkv, qkv, qkv, qkv, qkv, bias)


def _t5_bucket(dist):
    max_exact = NUM_BUCKETS // 2
    large = max_exact + (np.log(np.maximum(dist, max_exact) / max_exact)
                         / np.log(MAX_DISTANCE / max_exact)
                         * (NUM_BUCKETS - max_exact)).astype(np.int32)
    large = np.minimum(large, NUM_BUCKETS - 1)
    return np.where(dist < max_exact, dist, large).astype(np.int32)


def _attention_bias(rel_bias_group, dilation):
    w = ATTN_BLOCK
    buckets = _t5_bucket((w - np.arange(w + 1)) * dilation)
    onehot = jnp.asarray(np.eye(NUM_BUCKETS, dtype=np.float32)[buckets])
    per_c = jnp.einsum('cb,bh->hc', onehot, rel_bias_group.astype(F32),
                       precision=lax.Precision.HIGHEST) * LOG2E
    period = 2 * w + 1
    vec = jnp.concatenate([per_c, jnp.full((HEADS_PER_GROUP, period - (w + 1)), MASK_VALUE, F32)], axis=1)
    normal = jnp.tile(vec, (1, w))[:, :w * 2 * w].reshape(HEADS_PER_GROUP, w, 2 * w)
    first = jnp.where(np.arange(2 * w)[None, None, :] >= w, normal, MASK_VALUE)
    return jnp.stack([first, normal], axis=0)


def _retention_kernel(q_ref, k_ref, v_ref, g_ref, mask_ref, qdec_ref, kdec_ref, cdec_ref, wr_ref, o_ref,
                      state_ref, lhs_ref, acc_ref):
    @pl.when(pl.program_id(1) == 0)
    def _():
        state_ref[...] = jnp.zeros_like(state_ref)

    rows, chunk = q_ref.shape[0], q_ref.shape[1]
    qk_cols = [slice(h * RET_QK_DIM, (h + 1) * RET_QK_DIM) for h in range(RET_HEADS)]
    v_cols = [slice(h * RET_V_DIM, (h + 1) * RET_V_DIM) for h in range(RET_HEADS)]
    streams = [(b, h) for b in range(rows) for h in range(RET_HEADS)]
    for s, (b, h) in enumerate(streams):
        q = q_ref[b, :, qk_cols[h]]
        k = k_ref[b, :, qk_cols[h]]
        sc = lax.dot_general(q, k, (((1,), (1,)), ((), ())), preferred_element_type=F32) * mask_ref[h]
        lhs_ref[s, :, :chunk] = sc.astype(BF16)
        lhs_ref[s, :, chunk:] = (q.astype(F32) * qdec_ref[h]).astype(BF16)
    for s, (b, h) in enumerate(streams):
        rhs = jnp.concatenate([v_ref[b, :, v_cols[h]], state_ref[s].astype(BF16)], axis=0)
        acc_ref[b, :, v_cols[h]] = jnp.dot(lhs_ref[s], rhs, preferred_element_type=F32)
    for s, (b, h) in enumerate(streams):
        kd = (k_ref[b, :, qk_cols[h]].astype(F32) * kdec_ref[h]).astype(BF16)
        state_ref[s] = state_ref[s] * cdec_ref[h] + lax.dot_general(
            kd, v_ref[b, :, v_cols[h]], (((0,), (0,)), ((), ())), preferred_element_type=F32)
    for h in range(RET_HEADS):
        gated = []
        for b in range(rows):
            o = acc_ref[b, :, v_cols[h]]
            mu = jnp.mean(o, axis=-1, keepdims=True)
            cen = o - mu
            var = jnp.mean(cen * cen, axis=-1, keepdims=True)
            y = cen * lax.rsqrt(var + GN_EPS)
            gated.append(g_ref[b, :, v_cols[h]] * y.astype(BF16))
        part = jnp.dot(jnp.concatenate(gated, axis=0), wr_ref[v_cols[h], :], preferred_element_type=F32)
        proj = part if h == 0 else proj + part
    for b in range(rows):
        o_ref[b] = proj[b * chunk:(b + 1) * chunk].astype(o_ref.dtype)


def _retention_constants(chunk):
    log_g = np.log(1.0 - 2.0 ** (-5.0 - np.arange(RET_HEADS, dtype=np.float64)))
    n = np.arange(chunk, dtype=np.float64)
    diff = n[:, None] - n[None, :]
    mask = np.where(diff >= 0, np.exp(log_g[:, None, None] * np.maximum(diff, 0.0)), 0.0)
    q_dec = np.exp(log_g[:, None] * (n + 1.0))
    k_dec = np.exp(log_g[:, None] * (chunk - 1.0 - n))
    c_dec = np.exp(log_g * chunk)
    return (jnp.asarray(mask, F32),
            jnp.asarray(np.broadcast_to(q_dec[:, :, None], (RET_HEADS, chunk, RET_QK_DIM)), F32),
            jnp.asarray(np.broadcast_to(k_dec[:, :, None], (RET_HEADS, chunk, RET_QK_DIM)), F32),
            jnp.asarray(np.broadcast_to(c_dec[:, None, None], (RET_HEADS, 1, RET_V_DIM)), F32))


def _retention(rqk, rest, wr, batch, seq, chunk):
    consts = _retention_constants(chunk)
    rqk3 = rqk.reshape(batch, seq, 2 * RET_QK_WIDTH)
    rest3 = rest.reshape(batch, seq, rest.shape[1])
    rows = RET_BATCH_ROWS if batch % RET_BATCH_ROWS == 0 else 1
    out = pl.pallas_call(
        _retention_kernel,
        grid=(batch // rows, seq // chunk),
        in_specs=[pl.BlockSpec((rows, chunk, RET_QK_WIDTH), lambda b, c: (b, c, 0)),
                  pl.BlockSpec((rows, chunk, RET_QK_WIDTH), lambda b, c: (b, c, 1)),
                  pl.BlockSpec((rows, chunk, RET_V_WIDTH), lambda b, c: (b, c, 0)),
                  pl.BlockSpec((rows, chunk, RET_V_WIDTH), lambda b, c: (b, c, 1)),
                  *[_full(a) for a in consts], _resident(wr)],
        out_specs=pl.BlockSpec((rows, chunk, D_MODEL), lambda b, c: (b, c, 0)),
        out_shape=jax.ShapeDtypeStruct((batch, seq, D_MODEL), BF16),
        scratch_shapes=[pltpu.VMEM((rows * RET_HEADS, RET_QK_DIM, RET_V_DIM), F32),
                        pltpu.VMEM((rows * RET_HEADS, chunk, chunk + RET_QK_DIM), BF16),
                        pltpu.VMEM((rows, chunk, RET_V_WIDTH), F32)],
        compiler_params=_params("parallel", "arbitrary"),
        name="retention",
    )(rqk3, rqk3, rest3, rest3, *consts, wr)
    return out.reshape(batch * seq, D_MODEL)


def _layer_norm(y, g, b):
    mu = jnp.mean(y, axis=-1, keepdims=True)
    cen = y - mu
    var = jnp.mean(cen * cen, axis=-1, keepdims=True)
    return cen * lax.rsqrt(var + LN_EPS) * g + b


def _to_token_order(dst_ref, src_ref):
    dil, sub = src_ref.shape[1], src_ref.shape[2]
    for r in range(dil):
        for c in range(dst_ref.shape[0]):
            dst_ref[c, pl.ds(r, sub, stride=dil), :] = src_ref[0, r, :, c * LANES:(c + 1) * LANES].astype(F32)


def _slab_rows(ref, rows):
    return jnp.concatenate([ref[c, rows, :] for c in range(ref.shape[0])], axis=1)


def _merge_kernel(o0_ref, o1_ref, o2_ref, l0_ref, l1_ref, l2_ref, bb_ref, gates_ref, x_ref,
                  expand_ref, wa_ref, wo_ref, g_ref, b_ref, out_ref,
                  o1_tok, o2_tok, l1_tok, l2_tok):
    for dst, src in ((o1_tok, o1_ref), (o2_tok, o2_ref), (l1_tok, l1_ref), (l2_tok, l2_ref)):
        _to_token_order(dst, src)
    expand = expand_ref[...]
    sub = out_ref.shape[0] // MERGE_SUBTILES
    for t in range(MERGE_SUBTILES):
        rows = slice(t * sub, (t + 1) * sub)
        outs = (o0_ref[0, 0, rows, :].astype(F32), _slab_rows(o1_tok, rows), _slab_rows(o2_tok, rows))
        lses = (l0_ref[0, 0, rows, :], l1_tok[0, rows, :], l2_tok[0, rows, :])
        m = jnp.maximum(jnp.maximum(lses[0], lses[1]), lses[2])
        es = [jnp.exp(l - m) for l in lses]
        inv = 1.0 / (es[0] + es[1] + es[2])
        y_a = jnp.zeros(outs[0].shape, F32)
        for e, o in zip(es, outs):
            wgt = e * inv
            hi = wgt.astype(BF16)
            lo = (wgt - hi.astype(F32)).astype(BF16)
            wide = jnp.dot(jnp.concatenate([hi, lo], axis=1), expand, preferred_element_type=F32)
            y_a = y_a + wide * o
        gates = gates_ref[rows, :].astype(F32)
        branch_a = jnp.dot(y_a.astype(BF16), wa_ref[...], preferred_element_type=F32)
        merged = gates[:, :D_MODEL] * branch_a + gates[:, D_MODEL:] * bb_ref[rows, :].astype(F32)
        mix = jnp.dot(merged.astype(BF16), wo_ref[...], preferred_element_type=F32)
        out_ref[rows, :] = _layer_norm(ALPHA * x_ref[rows, :] + mix, g_ref[...], b_ref[...])


def _merge(os_, lses, branch_b, rest, x, wa, wo, g, b, seq, tm=1024):
    n = x.shape[0]
    tiles = seq // tm
    expand = jnp.asarray(
        (np.arange(2 * LANES)[:, None] % LANES == (np.arange(GROUP_WIDTH)[None, :] // HEAD_DIM)).astype(np.float32),
        BF16)
    gate_block = (2 * RET_V_WIDTH) // (2 * D_MODEL)
    row = lambda width: pl.BlockSpec((tm, width), lambda i: (i, 0))

    def strided(a):
        dil, width = a.shape[1], a.shape[3]
        return pl.BlockSpec((1, dil, tm // dil, width), lambda i: (i // tiles, 0, i % tiles, 0))

    return pl.pallas_call(
        _merge_kernel,
        grid=(n // tm,),
        in_specs=[strided(a) for a in os_] + [strided(a) for a in lses]
                 + [row(D_MODEL), pl.BlockSpec((tm, 2 * D_MODEL), lambda i: (i, gate_block)), row(D_MODEL),
                    _full(expand), _resident(wa), _resident(wo), _full(g), _full(b)],
        out_specs=row(D_MODEL),
        out_shape=jax.ShapeDtypeStruct((n, D_MODEL), F32),
        scratch_shapes=[pltpu.VMEM((GROUP_WIDTH // LANES, tm, LANES), F32),
                        pltpu.VMEM((GROUP_WIDTH // LANES, tm, LANES), F32),
                        pltpu.VMEM((1, tm, LANES), F32), pltpu.VMEM((1, tm, LANES), F32)],
        compiler_params=_params("parallel"),
        name="merge",
    )(*os_, *lses, branch_b, rest, x, expand, wa, wo, g, b)


def _ffn_kernel(x_ref, wg_ref, wu_ref, wd_ref, g_ref, b_ref, out_ref):
    sub = out_ref.shape[0] // FFN_SUBTILES
    for t in range(FFN_SUBTILES):
        rows = slice(t * sub, (t + 1) * sub)
        x = x_ref[rows, :]
        xb = x.astype(BF16)
        gate = jnp.dot(xb, wg_ref[...], preferred_element_type=F32)
        up = jnp.dot(xb, wu_ref[...], preferred_element_type=F32)
        hidden = (gate * _sigmoid(gate) * up).astype(BF16)
        ffn = jnp.dot(hidden, wd_ref[...], preferred_element_type=F32)
        out_ref[rows, :] = _layer_norm(ALPHA * x + ffn, g_ref[...], b_ref[...])


def _ffn(x, wg, wu, wd, g, b, tm=1024):
    n = x.shape[0]
    row = pl.BlockSpec((tm, D_MODEL), lambda i: (i, 0))
    return pl.pallas_call(
        _ffn_kernel,
        grid=(n // tm,),
        in_specs=[row, _resident(wg), _resident(wu), _resident(wd), _full(g), _full(b)],
        out_specs=row,
        out_shape=jax.ShapeDtypeStruct((n, D_MODEL), F32),
        compiler_params=_params("parallel"),
        name="ffn",
    )(x, wg, wu, wd, g, b)


def _rope_tables(seq):
    half = RET_QK_DIM // 2
    inv_freq = ROPE_BASE ** (-np.arange(half, dtype=np.float64) / half)
    ang = np.arange(seq, dtype=np.float64)[:, None] * inv_freq[None]
    return jnp.asarray(np.cos(ang), F32), jnp.asarray(np.sin(ang), F32)


def _group_major(a):
    parts = [a[..., (t * N_GROUPS + gi) * GROUP_WIDTH:(t * N_GROUPS + gi + 1) * GROUP_WIDTH]
             for gi in range(N_GROUPS) for t in range(3)]
    return jnp.concatenate(parts, axis=-1)


def kernel(x, rel_bias, w_in, b_in, w_attn_proj, w_ret_proj, w_out, ln1_g, ln1_b,
           w_ffn_gate, w_ffn_up, w_ffn_down, ln2_g, ln2_b):
    batch, seq, d = x.shape
    n = batch * seq
    assert d == D_MODEL and seq % 2048 == 0
    cos, sin = _rope_tables(seq)
    biases = [_attention_bias(rel_bias[:, gi * HEADS_PER_GROUP:(gi + 1) * HEADS_PER_GROUP], dil)
              for gi, (_, dil) in enumerate(ATTN_GROUPS)]
    xf = x.reshape(n, d)
    for l in range(DEPTH):
        w = w_in[l]
        bias_in = b_in[l].reshape(1, -1)
        *qkvs, rqk, rest = _project(
            xf, _group_major(w[:, :_ATTN_COLS]).astype(BF16), w[:, _COL_RQK[0]:_COL_RQK[1]].astype(BF16),
            w[:, _COL_REST[0]:_COL_REST[1]].astype(BF16),
            _group_major(bias_in[:, :_ATTN_COLS]), bias_in[:, _COL_RQK[0]:_COL_RQK[1]],
            bias_in[:, _COL_REST[0]:_COL_REST[1]], cos, sin, batch, seq)
        os_, lses = [], []
        for qkv, bias in zip(qkvs, biases):
            o, lse = _attention_group(qkv, bias, step_rows=2048)
            os_.append(o)
            lses.append(lse)
        branch_b = _retention(rqk, rest, w_ret_proj[l].astype(BF16), batch, seq, RET_CHUNK)
        xf = _merge(os_, lses, branch_b, rest, xf,
                    w_attn_proj[l].astype(BF16), w_out[l].astype(BF16),
                    ln1_g[l].reshape(1, d), ln1_b[l].reshape(1, d), seq)
        xf = _ffn(xf, w_ffn_gate[l].astype(BF16), w_ffn_up[l].astype(BF16), w_ffn_down[l].astype(BF16),
                  ln2_g[l].reshape(1, d), ln2_b[l].reshape(1, d))
    return xf.reshape(batch, seq, d)
```

```python
import functools

import numpy as np
import jax
import jax.numpy as jnp
from jax import lax
from jax.experimental import pallas as pl
from jax.experimental.pallas import tpu as pltpu

F32 = jnp.float32
BF16 = jnp.bfloat16

D_MODEL = 1024
DEPTH = 2
HEAD_DIM = 64
ATTN_GROUPS = ((128, 1), (512, 4), (2048, 16))
N_GROUPS = len(ATTN_GROUPS)
HEADS_PER_GROUP = 6
GROUP_WIDTH = HEADS_PER_GROUP * HEAD_DIM
QKV_WIDTH = 3 * GROUP_WIDTH
NUM_BUCKETS = 32
MAX_DISTANCE = 2048
ATTN_BLOCK = 128
LANES = 128
RET_HEADS = 4
RET_QK_DIM = 256
RET_V_DIM = 512
RET_QK_WIDTH = RET_HEADS * RET_QK_DIM
RET_V_WIDTH = RET_HEADS * RET_V_DIM
RET_CHUNK = 256
RET_BATCH_ROWS = 2
ROPE_BASE = 10000.0
D_FF = 2816
ALPHA = (2 * DEPTH) ** 0.25
LN_EPS = 1e-5
GN_EPS = 1e-5
MASK_VALUE = -1e30
LOG2E = 1.4426950408889634
LN2 = 0.6931471805599453

_ATTN_COLS = 3 * N_GROUPS * GROUP_WIDTH
_COL_RQK = (_ATTN_COLS, _ATTN_COLS + 2 * RET_QK_WIDTH)
_COL_REST = (_COL_RQK[1], _COL_RQK[1] + 2 * RET_V_WIDTH + 2 * D_MODEL)
_REST_CHUNK = 512

MERGE_SUBTILES = 4
FFN_SUBTILES = 4

VMEM_LIMIT = 60 * 1024 * 1024


def _params(*sem):
    return pltpu.CompilerParams(dimension_semantics=sem, vmem_limit_bytes=VMEM_LIMIT)


def _full(a):
    return pl.BlockSpec(a.shape, lambda *_: (0,) * a.ndim)


def _resident(a):
    return pl.BlockSpec(a.shape, lambda *_: (0,) * a.ndim, pipeline_mode=pl.Buffered(1))


def _sigmoid(x):
    return 0.5 * jnp.tanh(0.5 * x) + 0.5


def _proj_rest_kernel(x_ref, wr_ref, br_ref, rest_ref):
    xb = x_ref[...].astype(BF16)
    for lo in range(0, wr_ref.shape[1], _REST_CHUNK):
        acc = (jnp.dot(xb, wr_ref[:, lo:lo + _REST_CHUNK], preferred_element_type=F32)
               + br_ref[:, lo:lo + _REST_CHUNK])
        if lo >= 2 * RET_V_WIDTH:
            acc = _sigmoid(acc)
        elif lo >= RET_V_WIDTH:
            acc = acc * _sigmoid(acc)
        rest_ref[:, lo:lo + _REST_CHUNK] = acc.astype(rest_ref.dtype)


def _proj_qkv_kernel(x_ref, wa_ref, wq_ref, ba_ref, bq_ref, cos_ref, sin_ref,
                     o0_ref, o1_ref, o2_ref, rqk_ref, stage_ref):
    tm = x_ref.shape[0]
    xb = x_ref[...].astype(BF16)

    half = RET_QK_DIM // 2
    for qk, scale in ((0, 1.0), (1, RET_QK_DIM ** -0.5)):
        c = cos_ref[...] * scale
        s = sin_ref[...] * scale
        for h in range(RET_HEADS):
            lo = qk * RET_QK_WIDTH + h * RET_QK_DIM
            acc = (jnp.dot(xb, wq_ref[:, lo:lo + RET_QK_DIM], preferred_element_type=F32)
                   + bq_ref[:, lo:lo + RET_QK_DIM])
            t1 = acc[:, :half]
            t2 = acc[:, half:]
            rqk_ref[:, lo:lo + half] = (t1 * c - t2 * s).astype(rqk_ref.dtype)
            rqk_ref[:, lo + half:lo + RET_QK_DIM] = (t1 * s + t2 * c).astype(rqk_ref.dtype)

    pair_cols = slice(0, 2 * QKV_WIDTH)
    acc01 = jnp.dot(xb, wa_ref[:, pair_cols], preferred_element_type=F32) + ba_ref[:, pair_cols]
    last_cols = slice(2 * QKV_WIDTH, 3 * QKV_WIDTH)
    acc2 = jnp.dot(xb, wa_ref[:, last_cols], preferred_element_type=F32) + ba_ref[:, last_cols]
    accs = (acc01[:, :QKV_WIDTH], acc01[:, QKV_WIDTH:], acc2)
    for acc, (_, dil), o_ref in zip(accs, ATTN_GROUPS, (o0_ref, o1_ref, o2_ref)):
        sub = tm // dil
        q = acc[:, :GROUP_WIDTH] * (HEAD_DIM ** -0.5 * LOG2E)
        if dil == 1:
            o_ref[0, 0, :, :GROUP_WIDTH] = q.astype(o_ref.dtype)
            o_ref[0, 0, :, GROUP_WIDTH:] = acc[:, GROUP_WIDTH:].astype(o_ref.dtype)
            continue
        for c in range(QKV_WIDTH // LANES):
            lanes = slice(c * LANES, (c + 1) * LANES)
            stage_ref[c] = q[:, lanes] if c < GROUP_WIDTH // LANES else acc[:, lanes]
        for r in range(dil):
            for c in range(QKV_WIDTH // LANES):
                o_ref[0, r, :, c * LANES:(c + 1) * LANES] = (
                    stage_ref[c, pl.ds(r, sub, stride=dil), :].astype(o_ref.dtype))


def _project(x, wa, wq, wr, ba, bq, br, cos, sin, batch, seq, tm=1024):
    n, d = x.shape
    tiles = seq // tm
    x_spec = pl.BlockSpec((tm, d), lambda i: (i, 0))
    outs, specs = [], []
    for _, dil in ATTN_GROUPS:
        outs.append(jax.ShapeDtypeStruct((batch, dil, seq // dil, QKV_WIDTH), BF16))
        specs.append(pl.BlockSpec((1, dil, tm // dil, QKV_WIDTH), lambda i: (i // tiles, 0, i % tiles, 0)))
    outs.append(jax.ShapeDtypeStruct((n, wq.shape[1]), BF16))
    specs.append(pl.BlockSpec((tm, wq.shape[1]), lambda i: (i, 0)))
    rot_spec = pl.BlockSpec((tm, RET_QK_DIM // 2), lambda i: (i % tiles, 0))
    qkv = pl.pallas_call(
        _proj_qkv_kernel,
        grid=(n // tm,),
        in_specs=[x_spec, _resident(wa), _resident(wq), _resident(ba), _resident(bq), rot_spec, rot_spec],
        out_specs=specs,
        out_shape=outs,
        scratch_shapes=[pltpu.VMEM((QKV_WIDTH // LANES, tm, LANES), F32)],
        compiler_params=_params("parallel"),
        name="proj_qkv",
    )(x, wa, wq, ba, bq, cos, sin)
    rest = pl.pallas_call(
        _proj_rest_kernel,
        grid=(n // tm,),
        in_specs=[x_spec, _resident(wr), _resident(br)],
        out_specs=pl.BlockSpec((tm, wr.shape[1]), lambda i: (i, 0)),
        out_shape=jax.ShapeDtypeStruct((n, wr.shape[1]), BF16),
        compiler_params=_params("parallel"),
        name="proj_rest",
    )(x, wr, br)
    return (*qkv, rest)


def _attn_kernel(q_ref, kp_ref, kc_ref, vp_ref, vc_ref, bias_ref, o_ref, lse_ref, *, sub_blocks):
    first_step = pl.program_id(2) == 0
    w = ATTN_BLOCK
    lane = lax.broadcasted_iota(jnp.int32, (w, LANES), 1)
    low_half = lane < HEAD_DIM
    ones = jnp.ones((2 * w, LANES), BF16)
    for r in range(q_ref.shape[1]):
        for j in range(sub_blocks):
            rows = slice(j * w, (j + 1) * w)
            max_tile = jnp.zeros((w, LANES), F32)
            denom_tile = jnp.ones((w, LANES), F32)
            for pair in range(HEADS_PER_GROUP // 2):
                cols = slice(pair * LANES, (pair + 1) * LANES)
                q = q_ref[0, r, rows, cols]
                if j == 0:
                    k = jnp.concatenate([kp_ref[0, r, :, cols], kc_ref[0, r, :w, cols]], axis=0)
                    v = jnp.concatenate([vp_ref[0, r, :, cols], vc_ref[0, r, :w, cols]], axis=0)
                else:
                    k = kc_ref[0, r, (j - 1) * w:(j + 1) * w, cols]
                    v = vc_ref[0, r, (j - 1) * w:(j + 1) * w, cols]
                v_ones = jnp.concatenate([v, ones], axis=1)
                halves = []
                for sel, hh in ((low_half, 0), (~low_half, 1)):
                    h = 2 * pair + hh
                    bias = bias_ref[jnp.where(first_step, 0, 1), h] if j == 0 else bias_ref[1, h]
                    qh = jnp.where(sel, q, jnp.zeros_like(q))
                    s = lax.dot_general(qh, k, (((1,), (1,)), ((), ())), preferred_element_type=F32) + bias
                    m = jnp.max(s, axis=-1, keepdims=True)
                    p = jnp.exp2(s - m).astype(BF16)
                    ov = jnp.dot(p, v_ones, preferred_element_type=F32)
                    halves.append(ov)
                    max_tile = jnp.where(lane == h, m, max_tile)
                    denom_tile = jnp.where(lane == h, ov[:, LANES:], denom_tile)
                numer = jnp.where(low_half, halves[0][:, :LANES], halves[1][:, :LANES])
                denom = jnp.where(low_half, halves[0][:, LANES:], halves[1][:, LANES:])
                o_ref[0, r, rows, cols] = (numer / denom).astype(o_ref.dtype)
            lse_ref[0, r, rows, :] = max_tile * LN2 + jnp.log(denom_tile)


def _attention_group(qkv, bias, step_rows):
    batch, dilation, sub_len, _ = qkv.shape
    q_rows = min(step_rows, sub_len)
    res = min(step_rows // q_rows, dilation)
    sub_blocks = q_rows // ATTN_BLOCK
    steps = sub_len // q_rows

    def cur(unit):
        return pl.BlockSpec((1, res, q_rows, GROUP_WIDTH), lambda b, r, n: (b, r, n, unit))

    def prev(unit):
        return pl.BlockSpec((1, res, ATTN_BLOCK, GROUP_WIDTH),
                            lambda b, r, n: (b, r, jnp.maximum(n * sub_blocks - 1, 0), unit))

    return pl.pallas_call(
        functools.partial(_attn_kernel, sub_blocks=sub_blocks),
        grid=(batch, dilation // res, steps),
        in_specs=[cur(0), prev(1), cur(1), prev(2), cur(2), _full(bias)],
        out_specs=[pl.BlockSpec((1, res, q_rows, GROUP_WIDTH), lambda b, r, n: (b, r, n, 0)),
                   pl.BlockSpec((1, res, q_rows, LANES), lambda b, r, n: (b, r, n, 0))],
        out_shape=[jax.ShapeDtypeStruct((batch, dilation, sub_len, GROUP_WIDTH), BF16),
                   jax.ShapeDtypeStruct((batch, dilation, sub_len, LANES), F32)],
        compiler_params=_params("parallel", "parallel", "arbitrary"),
        name=f"attn_d{dilation}",
    )(qkv, qkv, qkv, qkv, qkv, bias)


def _t5_bucket(dist):
    max_exact = NUM_BUCKETS // 2
    large = max_exact + (np.log(np.maximum(dist, max_exact) / max_exact)
                         / np.log(MAX_DISTANCE / max_exact)
                         * (NUM_BUCKETS - max_exact)).astype(np.int32)
    large = np.minimum(large, NUM_BUCKETS - 1)
    return np.where(dist < max_exact, dist, large).astype(np.int32)


def _attention_bias(rel_bias_group, dilation):
    w = ATTN_BLOCK
    buckets = _t5_bucket((w - np.arange(w + 1)) * dilation)
    onehot = jnp.asarray(np.eye(NUM_BUCKETS, dtype=np.float32)[buckets])
    per_c = jnp.einsum('cb,bh->hc', onehot, rel_bias_group.astype(F32),
                       precision=lax.Precision.HIGHEST) * LOG2E
    period = 2 * w + 1
    vec = jnp.concatenate([per_c, jnp.full((HEADS_PER_GROUP, period - (w + 1)), MASK_VALUE, F32)], axis=1)
    normal = jnp.tile(vec, (1, w))[:, :w * 2 * w].reshape(HEADS_PER_GROUP, w, 2 * w)
    first = jnp.where(np.arange(2 * w)[None, None, :] >= w, normal, MASK_VALUE)
    return jnp.stack([first, normal], axis=0)


def _retention_kernel(q_ref, k_ref, v_ref, g_ref, mask_ref, qdec_ref, kdec_ref, cdec_ref, wr_ref, o_ref,
                      state_ref, lhs_ref, acc_ref):
    @pl.when(pl.program_id(1) == 0)
    def _():
        state_ref[...] = jnp.zeros_like(state_ref)

    rows, chunk = q_ref.shape[0], q_ref.shape[1]
    qk_cols = [slice(h * RET_QK_DIM, (h + 1) * RET_QK_DIM) for h in range(RET_HEADS)]
    v_cols = [slice(h * RET_V_DIM, (h + 1) * RET_V_DIM) for h in range(RET_HEADS)]
    streams = [(b, h) for b in range(rows) for h in range(RET_HEADS)]
    for s, (b, h) in enumerate(streams):
        q = q_ref[b, :, qk_cols[h]]
        k = k_ref[b, :, qk_cols[h]]
        sc = lax.dot_general(q, k, (((1,), (1,)), ((), ())), preferred_element_type=F32) * mask_ref[h]
        lhs_ref[s, :, :chunk] = sc.astype(BF16)
        lhs_ref[s, :, chunk:] = (q.astype(F32) * qdec_ref[h]).astype(BF16)
    for s, (b, h) in enumerate(streams):
        rhs = jnp.concatenate([v_ref[b, :, v_cols[h]], state_ref[s].astype(BF16)], axis=0)
        acc_ref[b, :, v_cols[h]] = jnp.dot(lhs_ref[s], rhs, preferred_element_type=F32)
    for s, (b, h) in enumerate(streams):
        kd = (k_ref[b, :, qk_cols[h]].astype(F32) * kdec_ref[h]).astype(BF16)
        state_ref[s] = state_ref[s] * cdec_ref[h] + lax.dot_general(
            kd, v_ref[b, :, v_cols[h]], (((0,), (0,)), ((), ())), preferred_element_type=F32)
    for h in range(RET_HEADS):
        gated = []
        for b in range(rows):
            o = acc_ref[b, :, v_cols[h]]
            mu = jnp.mean(o, axis=-1, keepdims=True)
            cen = o - mu
            var = jnp.mean(cen * cen, axis=-1, keepdims=True)
            y = cen * lax.rsqrt(var + GN_EPS)
            gated.append(g_ref[b, :, v_cols[h]] * y.astype(BF16))
        part = jnp.dot(jnp.concatenate(gated, axis=0), wr_ref[v_cols[h], :], preferred_element_type=F32)
        proj = part if h == 0 else proj + part
    for b in range(rows):
        o_ref[b] = proj[b * chunk:(b + 1) * chunk].astype(o_ref.dtype)


def _retention_constants(chunk):
    log_g = np.log(1.0 - 2.0 ** (-5.0 - np.arange(RET_HEADS, dtype=np.float64)))
    n = np.arange(chunk, dtype=np.float64)
    diff = n[:, None] - n[None, :]
    mask = np.where(diff >= 0, np.exp(log_g[:, None, None] * np.maximum(diff, 0.0)), 0.0)
    q_dec = np.exp(log_g[:, None] * (n + 1.0))
    k_dec = np.exp(log_g[:, None] * (chunk - 1.0 - n))
    c_dec = np.exp(log_g * chunk)
    return (jnp.asarray(mask, F32),
            jnp.asarray(np.broadcast_to(q_dec[:, :, None], (RET_HEADS, chunk, RET_QK_DIM)), F32),
            jnp.asarray(np.broadcast_to(k_dec[:, :, None], (RET_HEADS, chunk, RET_QK_DIM)), F32),
            jnp.asarray(np.broadcast_to(c_dec[:, None, None], (RET_HEADS, 1, RET_V_DIM)), F32))


def _retention(rqk, rest, wr, batch, seq, chunk):
    consts = _retention_constants(chunk)
    rqk3 = rqk.reshape(batch, seq, 2 * RET_QK_WIDTH)
    rest3 = rest.reshape(batch, seq, rest.shape[1])
    rows = RET_BATCH_ROWS if batch % RET_BATCH_ROWS == 0 else 1
    out = pl.pallas_call(
        _retention_kernel,
        grid=(batch // rows, seq // chunk),
        in_specs=[pl.BlockSpec((rows, chunk, RET_QK_WIDTH), lambda b, c: (b, c, 0)),
                  pl.BlockSpec((rows, chunk, RET_QK_WIDTH), lambda b, c: (b, c, 1)),
                  pl.BlockSpec((rows, chunk, RET_V_WIDTH), lambda b, c: (b, c, 0)),
                  pl.BlockSpec((rows, chunk, RET_V_WIDTH), lambda b, c: (b, c, 1)),
                  *[_full(a) for a in consts], _resident(wr)],
        out_specs=pl.BlockSpec((rows, chunk, D_MODEL), lambda b, c: (b, c, 0)),
        out_shape=jax.ShapeDtypeStruct((batch, seq, D_MODEL), BF16),
        scratch_shapes=[pltpu.VMEM((rows * RET_HEADS, RET_QK_DIM, RET_V_DIM), F32),
                        pltpu.VMEM((rows * RET_HEADS, chunk, chunk + RET_QK_DIM), BF16),
                        pltpu.VMEM((rows, chunk, RET_V_WIDTH), F32)],
        compiler_params=_params("parallel", "arbitrary"),
        name="retention",
    )(rqk3, rqk3, rest3, rest3, *consts, wr)
    return out.reshape(batch * seq, D_MODEL)


def _layer_norm(y, g, b):
    mu = jnp.mean(y, axis=-1, keepdims=True)
    cen = y - mu
    var = jnp.mean(cen * cen, axis=-1, keepdims=True)
    return cen * lax.rsqrt(var + LN_EPS) * g + b


def _to_token_order(dst_ref, src_ref):
    dil, sub = src_ref.shape[1], src_ref.shape[2]
    for r in range(dil):
        for c in range(dst_ref.shape[0]):
            dst_ref[c, pl.ds(r, sub, stride=dil), :] = src_ref[0, r, :, c * LANES:(c + 1) * LANES].astype(F32)


def _slab_rows(ref, rows):
    return jnp.concatenate([ref[c, rows, :] for c in range(ref.shape[0])], axis=1)


def _merge_kernel(o0_ref, o1_ref, o2_ref, l0_ref, l1_ref, l2_ref, bb_ref, gates_ref, x_ref,
                  expand_ref, wa_ref, wo_ref, g_ref, b_ref, out_ref,
                  o1_tok, o2_tok, l1_tok, l2_tok):
    for dst, src in ((o1_tok, o1_ref), (o2_tok, o2_ref), (l1_tok, l1_ref), (l2_tok, l2_ref)):
        _to_token_order(dst, src)
    expand = expand_ref[...]
    sub = out_ref.shape[0] // MERGE_SUBTILES
    for t in range(MERGE_SUBTILES):
        rows = slice(t * sub, (t + 1) * sub)
        outs = (o0_ref[0, 0, rows, :].astype(F32), _slab_rows(o1_tok, rows), _slab_rows(o2_tok, rows))
        lses = (l0_ref[0, 0, rows, :], l1_tok[0, rows, :], l2_tok[0, rows, :])
        m = jnp.maximum(jnp.maximum(lses[0], lses[1]), lses[2])
        es = [jnp.exp(l - m) for l in lses]
        inv = 1.0 / (es[0] + es[1] + es[2])
        y_a = jnp.zeros(outs[0].shape, F32)
        for e, o in zip(es, outs):
            wgt = e * inv
            hi = wgt.astype(BF16)
            lo = (wgt - hi.astype(F32)).astype(BF16)
            wide = jnp.dot(jnp.concatenate([hi, lo], axis=1), expand, preferred_element_type=F32)
            y_a = y_a + wide * o
        gates = gates_ref[rows, :].astype(F32)
        branch_a = jnp.dot(y_a.astype(BF16), wa_ref[...], preferred_element_type=F32)
        merged = gates[:, :D_MODEL] * branch_a + gates[:, D_MODEL:] * bb_ref[rows, :].astype(F32)
        mix = jnp.dot(merged.astype(BF16), wo_ref[...], preferred_element_type=F32)
        out_ref[rows, :] = _layer_norm(ALPHA * x_ref[rows, :] + mix, g_ref[...], b_ref[...])


def _merge(os_, lses, branch_b, rest, x, wa, wo, g, b, seq, tm=1024):
    n = x.shape[0]
    tiles = seq // tm
    expand = jnp.asarray(
        (np.arange(2 * LANES)[:, None] % LANES == (np.arange(GROUP_WIDTH)[None, :] // HEAD_DIM)).astype(np.float32),
        BF16)
    gate_block = (2 * RET_V_WIDTH) // (2 * D_MODEL)
    row = lambda width: pl.BlockSpec((tm, width), lambda i: (i, 0))

    def strided(a):
        dil, width = a.shape[1], a.shape[3]
        return pl.BlockSpec((1, dil, tm // dil, width), lambda i: (i // tiles, 0, i % tiles, 0))

    return pl.pallas_call(
        _merge_kernel,
        grid=(n // tm,),
        in_specs=[strided(a) for a in os_] + [strided(a) for a in lses]
                 + [row(D_MODEL), pl.BlockSpec((tm, 2 * D_MODEL), lambda i: (i, gate_block)), row(D_MODEL),
                    _full(expand), _resident(wa), _resident(wo), _full(g), _full(b)],
        out_specs=row(D_MODEL),
        out_shape=jax.ShapeDtypeStruct((n, D_MODEL), F32),
        scratch_shapes=[pltpu.VMEM((GROUP_WIDTH // LANES, tm, LANES), F32),
                        pltpu.VMEM((GROUP_WIDTH // LANES, tm, LANES), F32),
                        pltpu.VMEM((1, tm, LANES), F32), pltpu.VMEM((1, tm, LANES), F32)],
        compiler_params=_params("parallel"),
        name="merge",
    )(*os_, *lses, branch_b, rest, x, expand, wa, wo, g, b)


def _ffn_kernel(x_ref, wg_ref, wu_ref, wd_ref, g_ref, b_ref, out_ref):
    sub = out_ref.shape[0] // FFN_SUBTILES
    for t in range(FFN_SUBTILES):
        rows = slice(t * sub, (t + 1) * sub)
        x = x_ref[rows, :]
        xb = x.astype(BF16)
        gate = jnp.dot(xb, wg_ref[...], preferred_element_type=F32)
        up = jnp.dot(xb, wu_ref[...], preferred_element_type=F32)
        hidden = (gate * _sigmoid(gate) * up).astype(BF16)
        ffn = jnp.dot(hidden, wd_ref[...], preferred_element_type=F32)
        out_ref[rows, :] = _layer_norm(ALPHA * x + ffn, g_ref[...], b_ref[...])


def _ffn(x, wg, wu, wd, g, b, tm=1024):
    n = x.shape[0]
    row = pl.BlockSpec((tm, D_MODEL), lambda i: (i, 0))
    return pl.pallas_call(
        _ffn_kernel,
        grid=(n // tm,),
        in_specs=[row, _resident(wg), _resident(wu), _resident(wd), _full(g), _full(b)],
        out_specs=row,
        out_shape=jax.ShapeDtypeStruct((n, D_MODEL), F32),
        compiler_params=_params("parallel"),
        name="ffn",
    )(x, wg, wu, wd, g, b)


def _rope_tables(seq):
    half = RET_QK_DIM // 2
    inv_freq = ROPE_BASE ** (-np.arange(half, dtype=np.float64) / half)
    ang = np.arange(seq, dtype=np.float64)[:, None] * inv_freq[None]
    return jnp.asarray(np.cos(ang), F32), jnp.asarray(np.sin(ang), F32)


def _group_major(a):
    parts = [a[..., (t * N_GROUPS + gi) * GROUP_WIDTH:(t * N_GROUPS + gi + 1) * GROUP_WIDTH]
             for gi in range(N_GROUPS) for t in range(3)]
    return jnp.concatenate(parts, axis=-1)


def kernel(x, rel_bias, w_in, b_in, w_attn_proj, w_ret_proj, w_out, ln1_g, ln1_b,
           w_ffn_gate, w_ffn_up, w_ffn_down, ln2_g, ln2_b):
    batch, seq, d = x.shape
    n = batch * seq
    assert d == D_MODEL and seq % 2048 == 0
    cos, sin = _rope_tables(seq)
    biases = [_attention_bias(rel_bias[:, gi * HEADS_PER_GROUP:(gi + 1) * HEADS_PER_GROUP], dil)
              for gi, (_, dil) in enumerate(ATTN_GROUPS)]
    xf = x.reshape(n, d)
    for l in range(DEPTH):
        w = w_in[l]
        bias_in = b_in[l].reshape(1, -1)
        *qkvs, rqk, rest = _project(
            xf, _group_major(w[:, :_ATTN_COLS]).astype(BF16), w[:, _COL_RQK[0]:_COL_RQK[1]].astype(BF16),
            w[:, _COL_REST[0]:_COL_REST[1]].astype(BF16),
            _group_major(bias_in[:, :_ATTN_COLS]), bias_in[:, _COL_RQK[0]:_COL_RQK[1]],
            bias_in[:, _COL_REST[0]:_COL_REST[1]], cos, sin, batch, seq)
        os_, lses = [], []
        for qkv, bias in zip(qkvs, biases):
            o, lse = _attention_group(qkv, bias, step_rows=2048)
            os_.append(o)
            lses.append(lse)
        branch_b = _retention(rqk, rest, w_ret_proj[l].astype(BF16), batch, seq, RET_CHUNK)
        xf = _merge(os_, lses, branch_b, rest, xf,
                    w_attn_proj[l].astype(BF16), w_out[l].astype(BF16),
                    ln1_g[l].reshape(1, d), ln1_b[l].reshape(1, d), seq)
        xf = _ffn(xf, w_ffn_gate[l].astype(BF16), w_ffn_up[l].astype(BF16), w_ffn_down[l].astype(BF16),
                  ln2_g[l].reshape(1, d), ln2_b[l].reshape(1, d))
    return xf.reshape(batch, seq, d)
```

```python
import functools

import numpy as np
import jax
import jax.numpy as jnp
from jax import lax
from jax.experimental import pallas as pl
from jax.experimental.pallas import tpu as pltpu

F32 = jnp.float32
BF16 = jnp.bfloat16

D_MODEL = 1024
DEPTH = 2
HEAD_DIM = 64
ATTN_GROUPS = ((128, 1), (512, 4), (2048, 16))
N_GROUPS = len(ATTN_GROUPS)
HEADS_PER_GROUP = 6
GROUP_WIDTH = HEADS_PER_GROUP * HEAD_DIM
QKV_WIDTH = 3 * GROUP_WIDTH
NUM_BUCKETS = 32
MAX_DISTANCE = 2048
ATTN_BLOCK = 128
LANES = 128
RET_HEADS = 4
RET_QK_DIM = 256
RET_V_DIM = 512
RET_QK_WIDTH = RET_HEADS * RET_QK_DIM
RET_V_WIDTH = RET_HEADS * RET_V_DIM
RET_CHUNK = 256
RET_BATCH_ROWS = 2
ROPE_BASE = 10000.0
D_FF = 2816
ALPHA = (2 * DEPTH) ** 0.25
LN_EPS = 1e-5
GN_EPS = 1e-5
MASK_VALUE = -1e30
LOG2E = 1.4426950408889634
LN2 = 0.6931471805599453

_ATTN_COLS = 3 * N_GROUPS * GROUP_WIDTH
_COL_RQK = (_ATTN_COLS, _ATTN_COLS + 2 * RET_QK_WIDTH)
_COL_REST = (_COL_RQK[1], _COL_RQK[1] + 2 * RET_V_WIDTH + 2 * D_MODEL)
_REST_CHUNK = 512

MERGE_SUBTILES = 2

VMEM_LIMIT = 60 * 1024 * 1024


def _params(*sem):
    return pltpu.CompilerParams(dimension_semantics=sem, vmem_limit_bytes=VMEM_LIMIT)


def _full(a):
    return pl.BlockSpec(a.shape, lambda *_: (0,) * a.ndim)


def _resident(a):
    return pl.BlockSpec(a.shape, lambda *_: (0,) * a.ndim, pipeline_mode=pl.Buffered(1))


def _sigmoid(x):
    return 0.5 * jnp.tanh(0.5 * x) + 0.5


def _proj_rest_kernel(x_ref, wr_ref, br_ref, rest_ref):
    xb = x_ref[...].astype(BF16)
    for lo in range(0, wr_ref.shape[1], _REST_CHUNK):
        acc = (jnp.dot(xb, wr_ref[:, lo:lo + _REST_CHUNK], preferred_element_type=F32)
               + br_ref[:, lo:lo + _REST_CHUNK])
        if lo >= 2 * RET_V_WIDTH:
            acc = _sigmoid(acc)
        elif lo >= RET_V_WIDTH:
            acc = acc * _sigmoid(acc)
        rest_ref[:, lo:lo + _REST_CHUNK] = acc.astype(rest_ref.dtype)


def _proj_qkv_kernel(x_ref, wa_ref, wq_ref, ba_ref, bq_ref, cos_ref, sin_ref,
                     o0_ref, o1_ref, o2_ref, rqk_ref, stage_ref):
    tm = x_ref.shape[0]
    xb = x_ref[...].astype(BF16)

    half = RET_QK_DIM // 2
    for qk, scale in ((0, 1.0), (1, RET_QK_DIM ** -0.5)):
        c = cos_ref[...] * scale
        s = sin_ref[...] * scale
        for h in range(RET_HEADS):
            lo = qk * RET_QK_WIDTH + h * RET_QK_DIM
            acc = (jnp.dot(xb, wq_ref[:, lo:lo + RET_QK_DIM], preferred_element_type=F32)
                   + bq_ref[:, lo:lo + RET_QK_DIM])
            t1 = acc[:, :half]
            t2 = acc[:, half:]
            rqk_ref[:, lo:lo + half] = (t1 * c - t2 * s).astype(rqk_ref.dtype)
            rqk_ref[:, lo + half:lo + RET_QK_DIM] = (t1 * s + t2 * c).astype(rqk_ref.dtype)

    pair_cols = slice(0, 2 * QKV_WIDTH)
    acc01 = jnp.dot(xb, wa_ref[:, pair_cols], preferred_element_type=F32) + ba_ref[:, pair_cols]
    last_cols = slice(2 * QKV_WIDTH, 3 * QKV_WIDTH)
    acc2 = jnp.dot(xb, wa_ref[:, last_cols], preferred_element_type=F32) + ba_ref[:, last_cols]
    accs = (acc01[:, :QKV_WIDTH], acc01[:, QKV_WIDTH:], acc2)
    for acc, (_, dil), o_ref in zip(accs, ATTN_GROUPS, (o0_ref, o1_ref, o2_ref)):
        sub = tm // dil
        q = acc[:, :GROUP_WIDTH] * (HEAD_DIM ** -0.5 * LOG2E)
        if dil == 1:
            o_ref[0, 0, :, :GROUP_WIDTH] = q.astype(o_ref.dtype)
            o_ref[0, 0, :, GROUP_WIDTH:] = acc[:, GROUP_WIDTH:].astype(o_ref.dtype)
            continue
        for c in range(QKV_WIDTH // LANES):
            lanes = slice(c * LANES, (c + 1) * LANES)
            stage_ref[c] = q[:, lanes] if c < GROUP_WIDTH // LANES else acc[:, lanes]
        for r in range(dil):
            for c in range(QKV_WIDTH // LANES):
                o_ref[0, r, :, c * LANES:(c + 1) * LANES] = (
                    stage_ref[c, pl.ds(r, sub, stride=dil), :].astype(o_ref.dtype))


def _project(x, wa, wq, wr, ba, bq, br, cos, sin, batch, seq, tm=1024):
    n, d = x.shape
    tiles = seq // tm
    x_spec = pl.BlockSpec((tm, d), lambda i: (i, 0))
    outs, specs = [], []
    for _, dil in ATTN_GROUPS:
        outs.append(jax.ShapeDtypeStruct((batch, dil, seq // dil, QKV_WIDTH), BF16))
        specs.append(pl.BlockSpec((1, dil, tm // dil, QKV_WIDTH), lambda i: (i // tiles, 0, i % tiles, 0)))
    outs.append(jax.ShapeDtypeStruct((n, wq.shape[1]), BF16))
    specs.append(pl.BlockSpec((tm, wq.shape[1]), lambda i: (i, 0)))
    rot_spec = pl.BlockSpec((tm, RET_QK_DIM // 2), lambda i: (i % tiles, 0))
    qkv = pl.pallas_call(
        _proj_qkv_kernel,
        grid=(n // tm,),
        in_specs=[x_spec, _resident(wa), _resident(wq), _resident(ba), _resident(bq), rot_spec, rot_spec],
        out_specs=specs,
        out_shape=outs,
        scratch_shapes=[pltpu.VMEM((QKV_WIDTH // LANES, tm, LANES), F32)],
        compiler_params=_params("parallel"),
        name="proj_qkv",
    )(x, wa, wq, ba, bq, cos, sin)
    rest = pl.pallas_call(
        _proj_rest_kernel,
        grid=(n // tm,),
        in_specs=[x_spec, _resident(wr), _resident(br)],
        out_specs=pl.BlockSpec((tm, wr.shape[1]), lambda i: (i, 0)),
        out_shape=jax.ShapeDtypeStruct((n, wr.shape[1]), BF16),
        compiler_params=_params("parallel"),
        name="proj_rest",
    )(x, wr, br)
    return (*qkv, rest)


def _attn_kernel(q_ref, kp_ref, kc_ref, vp_ref, vc_ref, bias_ref, o_ref, lse_ref, *, sub_blocks):
    first_step = pl.program_id(2) == 0
    w = ATTN_BLOCK
    lane = lax.broadcasted_iota(jnp.int32, (w, LANES), 1)
    low_half = lane < HEAD_DIM
    ones = jnp.ones((2 * w, LANES), BF16)
    for r in range(q_ref.shape[1]):
        for j in range(sub_blocks):
            rows = slice(j * w, (j + 1) * w)
            max_tile = jnp.zeros((w, LANES), F32)
            denom_tile = jnp.ones((w, LANES), F32)
            for pair in range(HEADS_PER_GROUP // 2):
                cols = slice(pair * LANES, (pair + 1) * LANES)
                q = q_ref[0, r, rows, cols]
                if j == 0:
                    k = jnp.concatenate([kp_ref[0, r, :, cols], kc_ref[0, r, :w, cols]], axis=0)
                    v = jnp.concatenate([vp_ref[0, r, :, cols], vc_ref[0, r, :w, cols]], axis=0)
                else:
                    k = kc_ref[0, r, (j - 1) * w:(j + 1) * w, cols]
                    v = vc_ref[0, r, (j - 1) * w:(j + 1) * w, cols]
                v_ones = jnp.concatenate([v, ones], axis=1)
                halves = []
                for sel, hh in ((low_half, 0), (~low_half, 1)):
                    h = 2 * pair + hh
                    bias = bias_ref[jnp.where(first_step, 0, 1), h] if j == 0 else bias_ref[1, h]
                    qh = jnp.where(sel, q, jnp.zeros_like(q))
                    s = lax.dot_general(qh, k, (((1,), (1,)), ((), ())), preferred_element_type=F32) + bias
                    m = jnp.max(s, axis=-1, keepdims=True)
                    p = jnp.exp2(s - m).astype(BF16)
                    ov = jnp.dot(p, v_ones, preferred_element_type=F32)
                    halves.append(ov)
                    max_tile = jnp.where(lane == h, m, max_tile)
                    denom_tile = jnp.where(lane == h, ov[:, LANES:], denom_tile)
                numer = jnp.where(low_half, halves[0][:, :LANES], halves[1][:, :LANES])
                denom = jnp.where(low_half, halves[0][:, LANES:], halves[1][:, LANES:])
                o_ref[0, r, rows, cols] = (numer / denom).astype(o_ref.dtype)
            lse_ref[0, r, rows, :] = max_tile * LN2 + jnp.log(denom_tile)


def _attention_group(qkv, bias, step_rows):
    batch, dilation, sub_len, _ = qkv.shape
    q_rows = min(step_rows, sub_len)
    res = min(step_rows // q_rows, dilation)
    sub_blocks = q_rows // ATTN_BLOCK
    steps = sub_len // q_rows

    def cur(unit):
        return pl.BlockSpec((1, res, q_rows, GROUP_WIDTH), lambda b, r, n: (b, r, n, unit))

    def prev(unit):
        return pl.BlockSpec((1, res, ATTN_BLOCK, GROUP_WIDTH),
                            lambda b, r, n: (b, r, jnp.maximum(n * sub_blocks - 1, 0), unit))

    return pl.pallas_call(
        functools.partial(_attn_kernel, sub_blocks=sub_blocks),
        grid=(batch, dilation // res, steps),
        in_specs=[cur(0), prev(1), cur(1), prev(2), cur(2), _full(bias)],
        out_specs=[pl.BlockSpec((1, res, q_rows, GROUP_WIDTH), lambda b, r, n: (b, r, n, 0)),
                   pl.BlockSpec((1, res, q_rows, LANES), lambda b, r, n: (b, r, n, 0))],
        out_shape=[jax.ShapeDtypeStruct((batch, dilation, sub_len, GROUP_WIDTH), BF16),
                   jax.ShapeDtypeStruct((batch, dilation, sub_len, LANES), F32)],
        compiler_params=_params("parallel", "parallel", "arbitrary"),
        name=f"attn_d{dilation}",
    )(qkv, qkv, qkv, qkv, qkv, bias)


def _t5_bucket(dist):
    max_exact = NUM_BUCKETS // 2
    large = max_exact + (np.log(np.maximum(dist, max_exact) / max_exact)
                         / np.log(MAX_DISTANCE / max_exact)
                         * (NUM_BUCKETS - max_exact)).astype(np.int32)
    large = np.minimum(large, NUM_BUCKETS - 1)
    return np.where(dist < max_exact, dist, large).astype(np.int32)


def _attention_bias(rel_bias_group, dilation):
    w = ATTN_BLOCK
    buckets = _t5_bucket((w - np.arange(w + 1)) * dilation)
    onehot = jnp.asarray(np.eye(NUM_BUCKETS, dtype=np.float32)[buckets])
    per_c = jnp.einsum('cb,bh->hc', onehot, rel_bias_group.astype(F32),
                       precision=lax.Precision.HIGHEST) * LOG2E
    period = 2 * w + 1
    vec = jnp.concatenate([per_c, jnp.full((HEADS_PER_GROUP, period - (w + 1)), MASK_VALUE, F32)], axis=1)
    normal = jnp.tile(vec, (1, w))[:, :w * 2 * w].reshape(HEADS_PER_GROUP, w, 2 * w)
    first = jnp.where(np.arange(2 * w)[None, None, :] >= w, normal, MASK_VALUE)
    return jnp.stack([first, normal], axis=0)


def _retention_kernel(q_ref, k_ref, v_ref, g_ref, mask_ref, qdec_ref, kdec_ref, cdec_ref, wr_ref, o_ref,
                      state_ref, lhs_ref, acc_ref):
    @pl.when(pl.program_id(1) == 0)
    def _():
        state_ref[...] = jnp.zeros_like(state_ref)

    rows, chunk = q_ref.shape[0], q_ref.shape[1]
    qk_cols = [slice(h * RET_QK_DIM, (h + 1) * RET_QK_DIM) for h in range(RET_HEADS)]
    v_cols = [slice(h * RET_V_DIM, (h + 1) * RET_V_DIM) for h in range(RET_HEADS)]
    streams = [(b, h) for b in range(rows) for h in range(RET_HEADS)]
    for s, (b, h) in enumerate(streams):
        q = q_ref[b, :, qk_cols[h]]
        k = k_ref[b, :, qk_cols[h]]
        sc = lax.dot_general(q, k, (((1,), (1,)), ((), ())), preferred_element_type=F32) * mask_ref[h]
        lhs_ref[s, :, :chunk] = sc.astype(BF16)
        lhs_ref[s, :, chunk:] = (q.astype(F32) * qdec_ref[h]).astype(BF16)
    for s, (b, h) in enumerate(streams):
        rhs = jnp.concatenate([v_ref[b, :, v_cols[h]], state_ref[s].astype(BF16)], axis=0)
        acc_ref[b, :, v_cols[h]] = jnp.dot(lhs_ref[s], rhs, preferred_element_type=F32)
    for s, (b, h) in enumerate(streams):
        kd = (k_ref[b, :, qk_cols[h]].astype(F32) * kdec_ref[h]).astype(BF16)
        state_ref[s] = state_ref[s] * cdec_ref[h] + lax.dot_general(
            kd, v_ref[b, :, v_cols[h]], (((0,), (0,)), ((), ())), preferred_element_type=F32)
    for h in range(RET_HEADS):
        gated = []
        for b in range(rows):
            o = acc_ref[b, :, v_cols[h]]
            mu = jnp.mean(o, axis=-1, keepdims=True)
            cen = o - mu
            var = jnp.mean(cen * cen, axis=-1, keepdims=True)
            y = cen * lax.rsqrt(var + GN_EPS)
            gated.append(g_ref[b, :, v_cols[h]] * y.astype(BF16))
        part = jnp.dot(jnp.concatenate(gated, axis=0), wr_ref[v_cols[h], :], preferred_element_type=F32)
        proj = part if h == 0 else proj + part
    for b in range(rows):
        o_ref[b] = proj[b * chunk:(b + 1) * chunk].astype(o_ref.dtype)


def _retention_constants(chunk):
    log_g = np.log(1.0 - 2.0 ** (-5.0 - np.arange(RET_HEADS, dtype=np.float64)))
    n = np.arange(chunk, dtype=np.float64)
    diff = n[:, None] - n[None, :]
    mask = np.where(diff >= 0, np.exp(log_g[:, None, None] * np.maximum(diff, 0.0)), 0.0)
    q_dec = np.exp(log_g[:, None] * (n + 1.0))
    k_dec = np.exp(log_g[:, None] * (chunk - 1.0 - n))
    c_dec = np.exp(log_g * chunk)
    return (jnp.asarray(mask, F32),
            jnp.asarray(np.broadcast_to(q_dec[:, :, None], (RET_HEADS, chunk, RET_QK_DIM)), F32),
            jnp.asarray(np.broadcast_to(k_dec[:, :, None], (RET_HEADS, chunk, RET_QK_DIM)), F32),
            jnp.asarray(np.broadcast_to(c_dec[:, None, None], (RET_HEADS, 1, RET_V_DIM)), F32))


def _retention(rqk, rest, wr, batch, seq, chunk):
    consts = _retention_constants(chunk)
    rqk3 = rqk.reshape(batch, seq, 2 * RET_QK_WIDTH)
    rest3 = rest.reshape(batch, seq, rest.shape[1])
    rows = RET_BATCH_ROWS if batch % RET_BATCH_ROWS == 0 else 1
    out = pl.pallas_call(
        _retention_kernel,
        grid=(batch // rows, seq // chunk),
        in_specs=[pl.BlockSpec((rows, chunk, RET_QK_WIDTH), lambda b, c: (b, c, 0)),
                  pl.BlockSpec((rows, chunk, RET_QK_WIDTH), lambda b, c: (b, c, 1)),
                  pl.BlockSpec((rows, chunk, RET_V_WIDTH), lambda b, c: (b, c, 0)),
                  pl.BlockSpec((rows, chunk, RET_V_WIDTH), lambda b, c: (b, c, 1)),
                  *[_full(a) for a in consts], _resident(wr)],
        out_specs=pl.BlockSpec((rows, chunk, D_MODEL), lambda b, c: (b, c, 0)),
        out_shape=jax.ShapeDtypeStruct((batch, seq, D_MODEL), BF16),
        scratch_shapes=[pltpu.VMEM((rows * RET_HEADS, RET_QK_DIM, RET_V_DIM), F32),
                        pltpu.VMEM((rows * RET_HEADS, chunk, chunk + RET_QK_DIM), BF16),
                        pltpu.VMEM((rows, chunk, RET_V_WIDTH), F32)],
        compiler_params=_params("parallel", "arbitrary"),
        name="retention",
    )(rqk3, rqk3, rest3, rest3, *consts, wr)
    return out.reshape(batch * seq, D_MODEL)


def _layer_norm(y, g, b):
    mu = jnp.mean(y, axis=-1, keepdims=True)
    cen = y - mu
    var = jnp.mean(cen * cen, axis=-1, keepdims=True)
    return cen * lax.rsqrt(var + LN_EPS) * g + b


def _to_token_order(dst_ref, src_ref):
    dil, sub = src_ref.shape[1], src_ref.shape[2]
    for r in range(dil):
        for c in range(dst_ref.shape[0]):
            dst_ref[c, pl.ds(r, sub, stride=dil), :] = src_ref[0, r, :, c * LANES:(c + 1) * LANES].astype(F32)


def _slab_rows(ref, rows):
    return jnp.concatenate([ref[c, rows, :] for c in range(ref.shape[0])], axis=1)


def _merge_ffn_kernel(o0_ref, o1_ref, o2_ref, l0_ref, l1_ref, l2_ref, bb_ref, gates_ref, x_ref,
                      expand_ref, wa_ref, wo_ref, g1_ref, b1_ref, wg_ref, wu_ref, wd_ref, g2_ref, b2_ref, out_ref,
                      o1_tok, o2_tok, l1_tok, l2_tok):
    for dst, src in ((o1_tok, o1_ref), (o2_tok, o2_ref), (l1_tok, l1_ref), (l2_tok, l2_ref)):
        _to_token_order(dst, src)
    expand = expand_ref[...]
    sub = out_ref.shape[0] // MERGE_SUBTILES
    mixed = []
    for t in range(MERGE_SUBTILES):
        rows = slice(t * sub, (t + 1) * sub)
        outs = (o0_ref[0, 0, rows, :].astype(F32), _slab_rows(o1_tok, rows), _slab_rows(o2_tok, rows))
        lses = (l0_ref[0, 0, rows, :], l1_tok[0, rows, :], l2_tok[0, rows, :])
        m = jnp.maximum(jnp.maximum(lses[0], lses[1]), lses[2])
        es = [jnp.exp(l - m) for l in lses]
        inv = 1.0 / (es[0] + es[1] + es[2])
        y_a = jnp.zeros(outs[0].shape, F32)
        for e, o in zip(es, outs):
            wgt = e * inv
            hi = wgt.astype(BF16)
            lo = (wgt - hi.astype(F32)).astype(BF16)
            wide = jnp.dot(jnp.concatenate([hi, lo], axis=1), expand, preferred_element_type=F32)
            y_a = y_a + wide * o
        gates = gates_ref[rows, :].astype(F32)
        branch_a = jnp.dot(y_a.astype(BF16), wa_ref[...], preferred_element_type=F32)
        merged = gates[:, :D_MODEL] * branch_a + gates[:, D_MODEL:] * bb_ref[rows, :].astype(F32)
        mix = jnp.dot(merged.astype(BF16), wo_ref[...], preferred_element_type=F32)
        mixed.append(_layer_norm(ALPHA * x_ref[rows, :] + mix, g1_ref[...], b1_ref[...]))
    for t, x1 in enumerate(mixed):
        rows = slice(t * sub, (t + 1) * sub)
        xb = x1.astype(BF16)
        gate = jnp.dot(xb, wg_ref[...], preferred_element_type=F32)
        up = jnp.dot(xb, wu_ref[...], preferred_element_type=F32)
        hidden = (gate * _sigmoid(gate) * up).astype(BF16)
        ffn = jnp.dot(hidden, wd_ref[...], preferred_element_type=F32)
        out_ref[rows, :] = _layer_norm(ALPHA * x1 + ffn, g2_ref[...], b2_ref[...])


def _merge_ffn(os_, lses, branch_b, rest, x, wa, wo, g1, b1, wg, wu, wd, g2, b2, seq, tm=512):
    n = x.shape[0]
    tiles = seq // tm
    expand = jnp.asarray(
        (np.arange(2 * LANES)[:, None] % LANES == (np.arange(GROUP_WIDTH)[None, :] // HEAD_DIM)).astype(np.float32),
        BF16)
    gate_block = (2 * RET_V_WIDTH) // (2 * D_MODEL)
    row = lambda width: pl.BlockSpec((tm, width), lambda i: (i, 0))

    def strided(a):
        dil, width = a.shape[1], a.shape[3]
        return pl.BlockSpec((1, dil, tm // dil, width), lambda i: (i // tiles, 0, i % tiles, 0))

    return pl.pallas_call(
        _merge_ffn_kernel,
        grid=(n // tm,),
        in_specs=[strided(a) for a in os_] + [strided(a) for a in lses]
                 + [row(D_MODEL), pl.BlockSpec((tm, 2 * D_MODEL), lambda i: (i, gate_block)), row(D_MODEL),
                    _full(expand), _resident(wa), _resident(wo), _full(g1), _full(b1),
                    _resident(wg), _resident(wu), _resident(wd), _full(g2), _full(b2)],
        out_specs=row(D_MODEL),
        out_shape=jax.ShapeDtypeStruct((n, D_MODEL), F32),
        scratch_shapes=[pltpu.VMEM((GROUP_WIDTH // LANES, tm, LANES), F32),
                        pltpu.VMEM((GROUP_WIDTH // LANES, tm, LANES), F32),
                        pltpu.VMEM((1, tm, LANES), F32), pltpu.VMEM((1, tm, LANES), F32)],
        compiler_params=_params("parallel"),
        name="merge_ffn",
    )(*os_, *lses, branch_b, rest, x, expand, wa, wo, g1, b1, wg, wu, wd, g2, b2)


def _rope_tables(seq):
    half = RET_QK_DIM // 2
    inv_freq = ROPE_BASE ** (-np.arange(half, dtype=np.float64) / half)
    ang = np.arange(seq, dtype=np.float64)[:, None] * inv_freq[None]
    return jnp.asarray(np.cos(ang), F32), jnp.asarray(np.sin(ang), F32)


def _group_major(a):
    parts = [a[..., (t * N_GROUPS + gi) * GROUP_WIDTH:(t * N_GROUPS + gi + 1) * GROUP_WIDTH]
             for gi in range(N_GROUPS) for t in range(3)]
    return jnp.concatenate(parts, axis=-1)


def kernel(x, rel_bias, w_in, b_in, w_attn_proj, w_ret_proj, w_out, ln1_g, ln1_b,
           w_ffn_gate, w_ffn_up, w_ffn_down, ln2_g, ln2_b):
    batch, seq, d = x.shape
    n = batch * seq
    assert d == D_MODEL and seq % 2048 == 0
    cos, sin = _rope_tables(seq)
    biases = [_attention_bias(rel_bias[:, gi * HEADS_PER_GROUP:(gi + 1) * HEADS_PER_GROUP], dil)
              for gi, (_, dil) in enumerate(ATTN_GROUPS)]
    xf = x.reshape(n, d)
    for l in range(DEPTH):
        w = w_in[l]
        bias_in = b_in[l].reshape(1, -1)
        *qkvs, rqk, rest = _project(
            xf, _group_major(w[:, :_ATTN_COLS]).astype(BF16), w[:, _COL_RQK[0]:_COL_RQK[1]].astype(BF16),
            w[:, _COL_REST[0]:_COL_REST[1]].astype(BF16),
            _group_major(bias_in[:, :_ATTN_COLS]), bias_in[:, _COL_RQK[0]:_COL_RQK[1]],
            bias_in[:, _COL_REST[0]:_COL_REST[1]], cos, sin, batch, seq)
        os_, lses = [], []
        for qkv, bias in zip(qkvs, biases):
            o, lse = _attention_group(qkv, bias, step_rows=2048)
            os_.append(o)
            lses.append(lse)
        branch_b = _retention(rqk, rest, w_ret_proj[l].astype(BF16), batch, seq, RET_CHUNK)
        xf = _merge_ffn(os_, lses, branch_b, rest, xf,
                        w_attn_proj[l].astype(BF16), w_out[l].astype(BF16),
                        ln1_g[l].reshape(1, d), ln1_b[l].reshape(1, d),
                        w_ffn_gate[l].astype(BF16), w_ffn_up[l].astype(BF16), w_ffn_down[l].astype(BF16),
                        ln2_g[l].reshape(1, d), ln2_b[l].reshape(1, d), seq)
    return xf.reshape(batch, seq, d)
```

```python
import functools

import numpy as np
import jax
import jax.numpy as jnp
from jax import lax
from jax.experimental import pallas as pl
from jax.experimental.pallas import tpu as pltpu

F32 = jnp.float32
BF16 = jnp.bfloat16

D_MODEL = 1024
DEPTH = 2
HEAD_DIM = 64
ATTN_GROUPS = ((128, 1), (512, 4), (2048, 16))
N_GROUPS = len(ATTN_GROUPS)
HEADS_PER_GROUP = 6
GROUP_WIDTH = HEADS_PER_GROUP * HEAD_DIM
QKV_WIDTH = 3 * GROUP_WIDTH
NUM_BUCKETS = 32
MAX_DISTANCE = 2048
ATTN_BLOCK = 128
LANES = 128
STRIDE_SPLIT = 4
RET_HEADS = 4
RET_QK_DIM = 256
RET_V_DIM = 512
RET_QK_WIDTH = RET_HEADS * RET_QK_DIM
RET_V_WIDTH = RET_HEADS * RET_V_DIM
RET_CHUNK = 256
RET_BATCH_ROWS = 2
ROPE_BASE = 10000.0
D_FF = 2816
ALPHA = (2 * DEPTH) ** 0.25
LN_EPS = 1e-5
GN_EPS = 1e-5
MASK_VALUE = -1e30
LOG2E = 1.4426950408889634
LN2 = 0.6931471805599453

_ATTN_COLS = 3 * N_GROUPS * GROUP_WIDTH
_COL_RQK = (_ATTN_COLS, _ATTN_COLS + 2 * RET_QK_WIDTH)
_COL_REST = (_COL_RQK[1], _COL_RQK[1] + 2 * RET_V_WIDTH + 2 * D_MODEL)
_REST_CHUNK = 512

MERGE_SUBTILES = 2

VMEM_LIMIT = 60 * 1024 * 1024


def _params(*sem):
    return pltpu.CompilerParams(dimension_semantics=sem, vmem_limit_bytes=VMEM_LIMIT)


def _full(a):
    return pl.BlockSpec(a.shape, lambda *_: (0,) * a.ndim)


def _resident(a):
    return pl.BlockSpec(a.shape, lambda *_: (0,) * a.ndim, pipeline_mode=pl.Buffered(1))


def _sigmoid(x):
    return 0.5 * jnp.tanh(0.5 * x) + 0.5


def _proj_rest_kernel(x_ref, wr_ref, br_ref, rest_ref):
    xb = x_ref[...].astype(BF16)
    for lo in range(0, wr_ref.shape[1], _REST_CHUNK):
        acc = (jnp.dot(xb, wr_ref[:, lo:lo + _REST_CHUNK], preferred_element_type=F32)
               + br_ref[:, lo:lo + _REST_CHUNK])
        if lo >= 2 * RET_V_WIDTH:
            acc = _sigmoid(acc)
        elif lo >= RET_V_WIDTH:
            acc = acc * _sigmoid(acc)
        rest_ref[:, lo:lo + _REST_CHUNK] = acc.astype(rest_ref.dtype)


def _proj_qkv_kernel(x_ref, wa_ref, wq_ref, ba_ref, bq_ref, cos_ref, sin_ref,
                     o0_ref, o1_ref, o2_ref, rqk_ref, stage_ref, stage2_ref):
    tm = x_ref.shape[0]
    xb = x_ref[...].astype(BF16)

    half = RET_QK_DIM // 2
    for qk, scale in ((0, 1.0), (1, RET_QK_DIM ** -0.5)):
        c = cos_ref[...] * scale
        s = sin_ref[...] * scale
        for h in range(RET_HEADS):
            lo = qk * RET_QK_WIDTH + h * RET_QK_DIM
            acc = (jnp.dot(xb, wq_ref[:, lo:lo + RET_QK_DIM], preferred_element_type=F32)
                   + bq_ref[:, lo:lo + RET_QK_DIM])
            t1 = acc[:, :half]
            t2 = acc[:, half:]
            rqk_ref[:, lo:lo + half] = (t1 * c - t2 * s).astype(rqk_ref.dtype)
            rqk_ref[:, lo + half:lo + RET_QK_DIM] = (t1 * s + t2 * c).astype(rqk_ref.dtype)

    pair_cols = slice(0, 2 * QKV_WIDTH)
    acc01 = jnp.dot(xb, wa_ref[:, pair_cols], preferred_element_type=F32) + ba_ref[:, pair_cols]
    last_cols = slice(2 * QKV_WIDTH, 3 * QKV_WIDTH)
    acc2 = jnp.dot(xb, wa_ref[:, last_cols], preferred_element_type=F32) + ba_ref[:, last_cols]
    accs = (acc01[:, :QKV_WIDTH], acc01[:, QKV_WIDTH:], acc2)
    for acc, (_, dil), o_ref in zip(accs, ATTN_GROUPS, (o0_ref, o1_ref, o2_ref)):
        sub = tm // dil
        q = acc[:, :GROUP_WIDTH] * (HEAD_DIM ** -0.5 * LOG2E)
        if dil == 1:
            o_ref[0, 0, :, :GROUP_WIDTH] = q.astype(o_ref.dtype)
            o_ref[0, 0, :, GROUP_WIDTH:] = acc[:, GROUP_WIDTH:].astype(o_ref.dtype)
            continue
        for c in range(QKV_WIDTH // LANES):
            lanes = slice(c * LANES, (c + 1) * LANES)
            stage_ref[c] = q[:, lanes] if c < GROUP_WIDTH // LANES else acc[:, lanes]
        if dil == STRIDE_SPLIT * STRIDE_SPLIT:
            quarter = tm // STRIDE_SPLIT
            for c in range(QKV_WIDTH // LANES):
                for r4 in range(STRIDE_SPLIT):
                    stage2_ref[c, r4 * quarter:(r4 + 1) * quarter, :] = (
                        stage_ref[c, pl.ds(r4, quarter, stride=STRIDE_SPLIT), :])
            for r in range(dil):
                r4, hi = r % STRIDE_SPLIT, r // STRIDE_SPLIT
                for c in range(QKV_WIDTH // LANES):
                    o_ref[0, r, :, c * LANES:(c + 1) * LANES] = (
                        stage2_ref[c, pl.ds(r4 * quarter + hi, sub, stride=STRIDE_SPLIT), :].astype(o_ref.dtype))
            continue
        for r in range(dil):
            for c in range(QKV_WIDTH // LANES):
                o_ref[0, r, :, c * LANES:(c + 1) * LANES] = (
                    stage_ref[c, pl.ds(r, sub, stride=dil), :].astype(o_ref.dtype))


def _project(x, wa, wq, wr, ba, bq, br, cos, sin, batch, seq, tm=1024):
    n, d = x.shape
    tiles = seq // tm
    x_spec = pl.BlockSpec((tm, d), lambda i: (i, 0))
    outs, specs = [], []
    for _, dil in ATTN_GROUPS:
        outs.append(jax.ShapeDtypeStruct((batch, dil, seq // dil, QKV_WIDTH), BF16))
        specs.append(pl.BlockSpec((1, dil, tm // dil, QKV_WIDTH), lambda i: (i // tiles, 0, i % tiles, 0)))
    outs.append(jax.ShapeDtypeStruct((n, wq.shape[1]), BF16))
    specs.append(pl.BlockSpec((tm, wq.shape[1]), lambda i: (i, 0)))
    rot_spec = pl.BlockSpec((tm, RET_QK_DIM // 2), lambda i: (i % tiles, 0))
    qkv = pl.pallas_call(
        _proj_qkv_kernel,
        grid=(n // tm,),
        in_specs=[x_spec, _resident(wa), _resident(wq), _resident(ba), _resident(bq), rot_spec, rot_spec],
        out_specs=specs,
        out_shape=outs,
        scratch_shapes=[pltpu.VMEM((QKV_WIDTH // LANES, tm, LANES), F32),
                        pltpu.VMEM((QKV_WIDTH // LANES, tm, LANES), F32)],
        compiler_params=_params("parallel"),
        name="proj_qkv",
    )(x, wa, wq, ba, bq, cos, sin)
    rest = pl.pallas_call(
        _proj_rest_kernel,
        grid=(n // tm,),
        in_specs=[x_spec, _resident(wr), _resident(br)],
        out_specs=pl.BlockSpec((tm, wr.shape[1]), lambda i: (i, 0)),
        out_shape=jax.ShapeDtypeStruct((n, wr.shape[1]), BF16),
        compiler_params=_params("parallel"),
        name="proj_rest",
    )(x, wr, br)
    return (*qkv, rest)


def _attn_kernel(q_ref, kp_ref, kc_ref, vp_ref, vc_ref, bias_ref, o_ref, lse_ref, *, sub_blocks):
    first_step = pl.program_id(2) == 0
    w = ATTN_BLOCK
    lane = lax.broadcasted_iota(jnp.int32, (w, LANES), 1)
    low_half = lane < HEAD_DIM
    ones = jnp.ones((2 * w, LANES), BF16)
    for r in range(q_ref.shape[1]):
        for j in range(sub_blocks):
            rows = slice(j * w, (j + 1) * w)
            max_tile = jnp.zeros((w, LANES), F32)
            denom_tile = jnp.ones((w, LANES), F32)
            for pair in range(HEADS_PER_GROUP // 2):
                cols = slice(pair * LANES, (pair + 1) * LANES)
                q = q_ref[0, r, rows, cols]
                if j == 0:
                    k = jnp.concatenate([kp_ref[0, r, :, cols], kc_ref[0, r, :w, cols]], axis=0)
                    v = jnp.concatenate([vp_ref[0, r, :, cols], vc_ref[0, r, :w, cols]], axis=0)
                else:
                    k = kc_ref[0, r, (j - 1) * w:(j + 1) * w, cols]
                    v = vc_ref[0, r, (j - 1) * w:(j + 1) * w, cols]
                v_ones = jnp.concatenate([v, ones], axis=1)
                halves = []
                for sel, hh in ((low_half, 0), (~low_half, 1)):
                    h = 2 * pair + hh
                    bias = bias_ref[jnp.where(first_step, 0, 1), h] if j == 0 else bias_ref[1, h]
                    qh = jnp.where(sel, q, jnp.zeros_like(q))
                    s = lax.dot_general(qh, k, (((1,), (1,)), ((), ())), preferred_element_type=F32) + bias
                    m = jnp.max(s, axis=-1, keepdims=True)
                    p = jnp.exp2(s - m).astype(BF16)
                    ov = jnp.dot(p, v_ones, preferred_element_type=F32)
                    halves.append(ov)
                    max_tile = jnp.where(lane == h, m, max_tile)
                    denom_tile = jnp.where(lane == h, ov[:, LANES:], denom_tile)
                numer = jnp.where(low_half, halves[0][:, :LANES], halves[1][:, :LANES])
                denom = jnp.where(low_half, halves[0][:, LANES:], halves[1][:, LANES:])
                o_ref[0, r, rows, cols] = (numer / denom).astype(o_ref.dtype)
            lse_ref[0, r, rows, :] = max_tile * LN2 + jnp.log(denom_tile)


def _attention_group(qkv, bias, step_rows):
    batch, dilation, sub_len, _ = qkv.shape
    q_rows = min(step_rows, sub_len)
    res = min(step_rows // q_rows, dilation)
    sub_blocks = q_rows // ATTN_BLOCK
    steps = sub_len // q_rows

    def cur(unit):
        return pl.BlockSpec((1, res, q_rows, GROUP_WIDTH), lambda b, r, n: (b, r, n, unit))

    def prev(unit):
        return pl.BlockSpec((1, res, ATTN_BLOCK, GROUP_WIDTH),
                            lambda b, r, n: (b, r, jnp.maximum(n * sub_blocks - 1, 0), unit))

    return pl.pallas_call(
        functools.partial(_attn_kernel, sub_blocks=sub_blocks),
        grid=(batch, dilation // res, steps),
        in_specs=[cur(0), prev(1), cur(1), prev(2), cur(2), _full(bias)],
        out_specs=[pl.BlockSpec((1, res, q_rows, GROUP_WIDTH), lambda b, r, n: (b, r, n, 0)),
                   pl.BlockSpec((1, res, q_rows, LANES), lambda b, r, n: (b, r, n, 0))],
        out_shape=[jax.ShapeDtypeStruct((batch, dilation, sub_len, GROUP_WIDTH), BF16),
                   jax.ShapeDtypeStruct((batch, dilation, sub_len, LANES), F32)],
        compiler_params=_params("parallel", "parallel", "arbitrary"),
        name=f"attn_d{dilation}",
    )(qkv, qkv, qkv, qkv, qkv, bias)


def _t5_bucket(dist):
    max_exact = NUM_BUCKETS // 2
    large = max_exact + (np.log(np.maximum(dist, max_exact) / max_exact)
                         / np.log(MAX_DISTANCE / max_exact)
                         * (NUM_BUCKETS - max_exact)).astype(np.int32)
    large = np.minimum(large, NUM_BUCKETS - 1)
    return np.where(dist < max_exact, dist, large).astype(np.int32)


def _attention_bias(rel_bias_group, dilation):
    w = ATTN_BLOCK
    buckets = _t5_bucket((w - np.arange(w + 1)) * dilation)
    onehot = jnp.asarray(np.eye(NUM_BUCKETS, dtype=np.float32)[buckets])
    per_c = jnp.einsum('cb,bh->hc', onehot, rel_bias_group.astype(F32),
                       precision=lax.Precision.HIGHEST) * LOG2E
    period = 2 * w + 1
    vec = jnp.concatenate([per_c, jnp.full((HEADS_PER_GROUP, period - (w + 1)), MASK_VALUE, F32)], axis=1)
    normal = jnp.tile(vec, (1, w))[:, :w * 2 * w].reshape(HEADS_PER_GROUP, w, 2 * w)
    first = jnp.where(np.arange(2 * w)[None, None, :] >= w, normal, MASK_VALUE)
    return jnp.stack([first, normal], axis=0)


def _retention_kernel(q_ref, k_ref, v_ref, g_ref, mask_ref, qdec_ref, kdec_ref, cdec_ref, wr_ref, o_ref,
                      state_ref, lhs_ref, acc_ref):
    @pl.when(pl.program_id(1) == 0)
    def _():
        state_ref[...] = jnp.zeros_like(state_ref)

    rows, chunk = q_ref.shape[0], q_ref.shape[1]
    qk_cols = [slice(h * RET_QK_DIM, (h + 1) * RET_QK_DIM) for h in range(RET_HEADS)]
    v_cols = [slice(h * RET_V_DIM, (h + 1) * RET_V_DIM) for h in range(RET_HEADS)]
    streams = [(b, h) for b in range(rows) for h in range(RET_HEADS)]
    for s, (b, h) in enumerate(streams):
        q = q_ref[b, :, qk_cols[h]]
        k = k_ref[b, :, qk_cols[h]]
        sc = lax.dot_general(q, k, (((1,), (1,)), ((), ())), preferred_element_type=F32) * mask_ref[h]
        lhs_ref[s, :, :chunk] = sc.astype(BF16)
        lhs_ref[s, :, chunk:] = (q.astype(F32) * qdec_ref[h]).astype(BF16)
    for s, (b, h) in enumerate(streams):
        rhs = jnp.concatenate([v_ref[b, :, v_cols[h]], state_ref[s].astype(BF16)], axis=0)
        acc_ref[b, :, v_cols[h]] = jnp.dot(lhs_ref[s], rhs, preferred_element_type=F32)
    for s, (b, h) in enumerate(streams):
        kd = (k_ref[b, :, qk_cols[h]].astype(F32) * kdec_ref[h]).astype(BF16)
        state_ref[s] = state_ref[s] * cdec_ref[h] + lax.dot_general(
            kd, v_ref[b, :, v_cols[h]], (((0,), (0,)), ((), ())), preferred_element_type=F32)
    for h in range(RET_HEADS):
        gated = []
        for b in range(rows):
            o = acc_ref[b, :, v_cols[h]]
            mu = jnp.mean(o, axis=-1, keepdims=True)
            cen = o - mu
            var = jnp.mean(cen * cen, axis=-1, keepdims=True)
            y = cen * lax.rsqrt(var + GN_EPS)
            gated.append(g_ref[b, :, v_cols[h]] * y.astype(BF16))
        part = jnp.dot(jnp.concatenate(gated, axis=0), wr_ref[v_cols[h], :], preferred_element_type=F32)
        proj = part if h == 0 else proj + part
    for b in range(rows):
        o_ref[b] = proj[b * chunk:(b + 1) * chunk].astype(o_ref.dtype)


def _retention_constants(chunk):
    log_g = np.log(1.0 - 2.0 ** (-5.0 - np.arange(RET_HEADS, dtype=np.float64)))
    n = np.arange(chunk, dtype=np.float64)
    diff = n[:, None] - n[None, :]
    mask = np.where(diff >= 0, np.exp(log_g[:, None, None] * np.maximum(diff, 0.0)), 0.0)
    q_dec = np.exp(log_g[:, None] * (n + 1.0))
    k_dec = np.exp(log_g[:, None] * (chunk - 1.0 - n))
    c_dec = np.exp(log_g * chunk)
    return (jnp.asarray(mask, F32),
            jnp.asarray(np.broadcast_to(q_dec[:, :, None], (RET_HEADS, chunk, RET_QK_DIM)), F32),
            jnp.asarray(np.broadcast_to(k_dec[:, :, None], (RET_HEADS, chunk, RET_QK_DIM)), F32),
            jnp.asarray(np.broadcast_to(c_dec[:, None, None], (RET_HEADS, 1, RET_V_DIM)), F32))


def _retention(rqk, rest, wr, batch, seq, chunk):
    consts = _retention_constants(chunk)
    rqk3 = rqk.reshape(batch, seq, 2 * RET_QK_WIDTH)
    rest3 = rest.reshape(batch, seq, rest.shape[1])
    rows = RET_BATCH_ROWS if batch % RET_BATCH_ROWS == 0 else 1
    out = pl.pallas_call(
        _retention_kernel,
        grid=(batch // rows, seq // chunk),
        in_specs=[pl.BlockSpec((rows, chunk, RET_QK_WIDTH), lambda b, c: (b, c, 0)),
                  pl.BlockSpec((rows, chunk, RET_QK_WIDTH), lambda b, c: (b, c, 1)),
                  pl.BlockSpec((rows, chunk, RET_V_WIDTH), lambda b, c: (b, c, 0)),
                  pl.BlockSpec((rows, chunk, RET_V_WIDTH), lambda b, c: (b, c, 1)),
                  *[_full(a) for a in consts], _resident(wr)],
        out_specs=pl.BlockSpec((rows, chunk, D_MODEL), lambda b, c: (b, c, 0)),
        out_shape=jax.ShapeDtypeStruct((batch, seq, D_MODEL), BF16),
        scratch_shapes=[pltpu.VMEM((rows * RET_HEADS, RET_QK_DIM, RET_V_DIM), F32),
                        pltpu.VMEM((rows * RET_HEADS, chunk, chunk + RET_QK_DIM), BF16),
                        pltpu.VMEM((rows, chunk, RET_V_WIDTH), F32)],
        compiler_params=_params("parallel", "arbitrary"),
        name="retention",
    )(rqk3, rqk3, rest3, rest3, *consts, wr)
    return out.reshape(batch * seq, D_MODEL)


def _layer_norm(y, g, b):
    mu = jnp.mean(y, axis=-1, keepdims=True)
    cen = y - mu
    var = jnp.mean(cen * cen, axis=-1, keepdims=True)
    return cen * lax.rsqrt(var + LN_EPS) * g + b


def _to_token_order(dst_ref, src_ref):
    dil, sub = src_ref.shape[1], src_ref.shape[2]
    for r in range(dil):
        for c in range(dst_ref.shape[0]):
            dst_ref[c, pl.ds(r, sub, stride=dil), :] = src_ref[0, r, :, c * LANES:(c + 1) * LANES].astype(F32)


def _slab_rows(ref, rows):
    return jnp.concatenate([ref[c, rows, :] for c in range(ref.shape[0])], axis=1)


def _merge_ffn_kernel(o0_ref, o1_ref, o2_ref, l0_ref, l1_ref, l2_ref, bb_ref, gates_ref, x_ref,
                      expand_ref, wa_ref, wo_ref, g1_ref, b1_ref, wg_ref, wu_ref, wd_ref, g2_ref, b2_ref, out_ref,
                      o1_tok, o2_tok, l1_tok, l2_tok):
    for dst, src in ((o1_tok, o1_ref), (o2_tok, o2_ref), (l1_tok, l1_ref), (l2_tok, l2_ref)):
        _to_token_order(dst, src)
    expand = expand_ref[...]
    sub = out_ref.shape[0] // MERGE_SUBTILES
    mixed = []
    for t in range(MERGE_SUBTILES):
        rows = slice(t * sub, (t + 1) * sub)
        outs = (o0_ref[0, 0, rows, :].astype(F32), _slab_rows(o1_tok, rows), _slab_rows(o2_tok, rows))
        lses = (l0_ref[0, 0, rows, :], l1_tok[0, rows, :], l2_tok[0, rows, :])
        m = jnp.maximum(jnp.maximum(lses[0], lses[1]), lses[2])
        es = [jnp.exp(l - m) for l in lses]
        inv = 1.0 / (es[0] + es[1] + es[2])
        y_a = jnp.zeros(outs[0].shape, F32)
        for e, o in zip(es, outs):
            wgt = e * inv
            hi = wgt.astype(BF16)
            lo = (wgt - hi.astype(F32)).astype(BF16)
            wide = jnp.dot(jnp.concatenate([hi, lo], axis=1), expand, preferred_element_type=F32)
            y_a = y_a + wide * o
        gates = gates_ref[rows, :].astype(F32)
        branch_a = jnp.dot(y_a.astype(BF16), wa_ref[...], preferred_element_type=F32)
        merged = gates[:, :D_MODEL] * branch_a + gates[:, D_MODEL:] * bb_ref[rows, :].astype(F32)
        mix = jnp.dot(merged.astype(BF16), wo_ref[...], preferred_element_type=F32)
        mixed.append(_layer_norm(ALPHA * x_ref[rows, :] + mix, g1_ref[...], b1_ref[...]))
    for t, x1 in enumerate(mixed):
        rows = slice(t * sub, (t + 1) * sub)
        xb = x1.astype(BF16)
        gate = jnp.dot(xb, wg_ref[...], preferred_element_type=F32)
        up = jnp.dot(xb, wu_ref[...], preferred_element_type=F32)
        hidden = (gate * _sigmoid(gate) * up).astype(BF16)
        ffn = jnp.dot(hidden, wd_ref[...], preferred_element_type=F32)
        out_ref[rows, :] = _layer_norm(ALPHA * x1 + ffn, g2_ref[...], b2_ref[...])


def _merge_ffn(os_, lses, branch_b, rest, x, wa, wo, g1, b1, wg, wu, wd, g2, b2, seq, tm=512):
    n = x.shape[0]
    tiles = seq // tm
    expand = jnp.asarray(
        (np.arange(2 * LANES)[:, None] % LANES == (np.arange(GROUP_WIDTH)[None, :] // HEAD_DIM)).astype(np.float32),
        BF16)
    gate_block = (2 * RET_V_WIDTH) // (2 * D_MODEL)
    row = lambda width: pl.BlockSpec((tm, width), lambda i: (i, 0))

    def strided(a):
        dil, width = a.shape[1], a.shape[3]
        return pl.BlockSpec((1, dil, tm // dil, width), lambda i: (i // tiles, 0, i % tiles, 0))

    return pl.pallas_call(
        _merge_ffn_kernel,
        grid=(n // tm,),
        in_specs=[strided(a) for a in os_] + [strided(a) for a in lses]
                 + [row(D_MODEL), pl.BlockSpec((tm, 2 * D_MODEL), lambda i: (i, gate_block)), row(D_MODEL),
                    _full(expand), _resident(wa), _resident(wo), _full(g1), _full(b1),
                    _resident(wg), _resident(wu), _resident(wd), _full(g2), _full(b2)],
        out_specs=row(D_MODEL),
        out_shape=jax.ShapeDtypeStruct((n, D_MODEL), F32),
        scratch_shapes=[pltpu.VMEM((GROUP_WIDTH // LANES, tm, LANES), F32),
                        pltpu.VMEM((GROUP_WIDTH // LANES, tm, LANES), F32),
                        pltpu.VMEM((1, tm, LANES), F32), pltpu.VMEM((1, tm, LANES), F32)],
        compiler_params=_params("parallel"),
        name="merge_ffn",
    )(*os_, *lses, branch_b, rest, x, expand, wa, wo, g1, b1, wg, wu, wd, g2, b2)


def _rope_tables(seq):
    half = RET_QK_DIM // 2
    inv_freq = ROPE_BASE ** (-np.arange(half, dtype=np.float64) / half)
    ang = np.arange(seq, dtype=np.float64)[:, None] * inv_freq[None]
    return jnp.asarray(np.cos(ang), F32), jnp.asarray(np.sin(ang), F32)


def _group_major(a):
    parts = [a[..., (t * N_GROUPS + gi) * GROUP_WIDTH:(t * N_GROUPS + gi + 1) * GROUP_WIDTH]
             for gi in range(N_GROUPS) for t in range(3)]
    return jnp.concatenate(parts, axis=-1)


def kernel(x, rel_bias, w_in, b_in, w_attn_proj, w_ret_proj, w_out, ln1_g, ln1_b,
           w_ffn_gate, w_ffn_up, w_ffn_down, ln2_g, ln2_b):
    batch, seq, d = x.shape
    n = batch * seq
    assert d == D_MODEL and seq % 2048 == 0
    cos, sin = _rope_tables(seq)
    biases = [_attention_bias(rel_bias[:, gi * HEADS_PER_GROUP:(gi + 1) * HEADS_PER_GROUP], dil)
              for gi, (_, dil) in enumerate(ATTN_GROUPS)]
    xf = x.reshape(n, d)
    for l in range(DEPTH):
        w = w_in[l]
        bias_in = b_in[l].reshape(1, -1)
        *qkvs, rqk, rest = _project(
            xf, _group_major(w[:, :_ATTN_COLS]).astype(BF16), w[:, _COL_RQK[0]:_COL_RQK[1]].astype(BF16),
            w[:, _COL_REST[0]:_COL_REST[1]].astype(BF16),
            _group_major(bias_in[:, :_ATTN_COLS]), bias_in[:, _COL_RQK[0]:_COL_RQK[1]],
            bias_in[:, _COL_REST[0]:_COL_REST[1]], cos, sin, batch, seq)
        os_, lses = [], []
        for qkv, bias in zip(qkvs, biases):
            o, lse = _attention_group(qkv, bias, step_rows=2048)
            os_.append(o)
            lses.append(lse)
        branch_b = _retention(rqk, rest, w_ret_proj[l].astype(BF16), batch, seq, RET_CHUNK)
        xf = _merge_ffn(os_, lses, branch_b, rest, xf,
                        w_attn_proj[l].astype(BF16), w_out[l].astype(BF16),
                        ln1_g[l].reshape(1, d), ln1_b[l].reshape(1, d),
                        w_ffn_gate[l].astype(BF16), w_ffn_up[l].astype(BF16), w_ffn_down[l].astype(BF16),
                        ln2_g[l].reshape(1, d), ln2_b[l].reshape(1, d), seq)
    return xf.reshape(batch, seq, d)
```

```python
import functools

import numpy as np
import jax
import jax.numpy as jnp
from jax import lax
from jax.experimental import pallas as pl
from jax.experimental.pallas import tpu as pltpu

F32 = jnp.float32
BF16 = jnp.bfloat16

D_MODEL = 1024
DEPTH = 2
HEAD_DIM = 64
ATTN_GROUPS = ((128, 1), (512, 4), (2048, 16))
N_GROUPS = len(ATTN_GROUPS)
HEADS_PER_GROUP = 6
GROUP_WIDTH = HEADS_PER_GROUP * HEAD_DIM
QKV_WIDTH = 3 * GROUP_WIDTH
NUM_BUCKETS = 32
MAX_DISTANCE = 2048
ATTN_BLOCK = 128
LANES = 128
STRIDE_SPLIT = 4
RET_HEADS = 4
RET_QK_DIM = 256
RET_V_DIM = 512
RET_QK_WIDTH = RET_HEADS * RET_QK_DIM
RET_V_WIDTH = RET_HEADS * RET_V_DIM
RET_CHUNK = 256
RET_BATCH_ROWS = 2
ROPE_BASE = 10000.0
D_FF = 2816
ALPHA = (2 * DEPTH) ** 0.25
LN_EPS = 1e-5
GN_EPS = 1e-5
MASK_VALUE = -1e30
LOG2E = 1.4426950408889634
LN2 = 0.6931471805599453

_ATTN_COLS = 3 * N_GROUPS * GROUP_WIDTH
_COL_RQK = (_ATTN_COLS, _ATTN_COLS + 2 * RET_QK_WIDTH)
_COL_REST = (_COL_RQK[1], _COL_RQK[1] + 2 * RET_V_WIDTH + 2 * D_MODEL)
_REST_CHUNK = 512

MERGE_SUBTILES = 2

VMEM_LIMIT = 60 * 1024 * 1024


def _params(*sem):
    return pltpu.CompilerParams(dimension_semantics=sem, vmem_limit_bytes=VMEM_LIMIT)


def _full(a):
    return pl.BlockSpec(a.shape, lambda *_: (0,) * a.ndim)


def _resident(a):
    return pl.BlockSpec(a.shape, lambda *_: (0,) * a.ndim, pipeline_mode=pl.Buffered(1))


def _sigmoid(x):
    return 0.5 * jnp.tanh(0.5 * x) + 0.5


def _proj_rest_kernel(x_ref, wr_ref, br_ref, rest_ref):
    xb = x_ref[...].astype(BF16)
    for lo in range(0, wr_ref.shape[1], _REST_CHUNK):
        acc = (jnp.dot(xb, wr_ref[:, lo:lo + _REST_CHUNK], preferred_element_type=F32)
               + br_ref[:, lo:lo + _REST_CHUNK])
        if lo >= 2 * RET_V_WIDTH:
            acc = _sigmoid(acc)
        elif lo >= RET_V_WIDTH:
            acc = acc * _sigmoid(acc)
        rest_ref[:, lo:lo + _REST_CHUNK] = acc.astype(rest_ref.dtype)


def _proj_qkv_kernel(x_ref, wa_ref, wq_ref, ba_ref, bq_ref, cos_ref, sin_ref,
                     o0_ref, o1_ref, o2_ref, rqk_ref, stage_ref, stage2_ref):
    tm = x_ref.shape[0]
    xb = x_ref[...].astype(BF16)

    half = RET_QK_DIM // 2
    for qk, scale in ((0, 1.0), (1, RET_QK_DIM ** -0.5)):
        c = cos_ref[...] * scale
        s = sin_ref[...] * scale
        for h in range(RET_HEADS):
            lo = qk * RET_QK_WIDTH + h * RET_QK_DIM
            acc = (jnp.dot(xb, wq_ref[:, lo:lo + RET_QK_DIM], preferred_element_type=F32)
                   + bq_ref[:, lo:lo + RET_QK_DIM])
            t1 = acc[:, :half]
            t2 = acc[:, half:]
            rqk_ref[:, lo:lo + half] = (t1 * c - t2 * s).astype(rqk_ref.dtype)
            rqk_ref[:, lo + half:lo + RET_QK_DIM] = (t1 * s + t2 * c).astype(rqk_ref.dtype)

    pair_cols = slice(0, 2 * QKV_WIDTH)
    acc01 = jnp.dot(xb, wa_ref[:, pair_cols], preferred_element_type=F32) + ba_ref[:, pair_cols]
    last_cols = slice(2 * QKV_WIDTH, 3 * QKV_WIDTH)
    acc2 = jnp.dot(xb, wa_ref[:, last_cols], preferred_element_type=F32) + ba_ref[:, last_cols]
    accs = (acc01[:, :QKV_WIDTH], acc01[:, QKV_WIDTH:], acc2)
    for acc, (_, dil), o_ref in zip(accs, ATTN_GROUPS, (o0_ref, o1_ref, o2_ref)):
        sub = tm // dil
        q = acc[:, :GROUP_WIDTH] * (HEAD_DIM ** -0.5 * LOG2E)
        if dil == 1:
            o_ref[0, 0, :, :GROUP_WIDTH] = q.astype(o_ref.dtype)
            o_ref[0, 0, :, GROUP_WIDTH:] = acc[:, GROUP_WIDTH:].astype(o_ref.dtype)
            continue
        for c in range(QKV_WIDTH // LANES):
            lanes = slice(c * LANES, (c + 1) * LANES)
            stage_ref[c] = q[:, lanes] if c < GROUP_WIDTH // LANES else acc[:, lanes]
        if dil == STRIDE_SPLIT * STRIDE_SPLIT:
            quarter = tm // STRIDE_SPLIT
            for c in range(QKV_WIDTH // LANES):
                for r4 in range(STRIDE_SPLIT):
                    stage2_ref[c, r4 * quarter:(r4 + 1) * quarter, :] = (
                        stage_ref[c, pl.ds(r4, quarter, stride=STRIDE_SPLIT), :])
            for r in range(dil):
                r4, hi = r % STRIDE_SPLIT, r // STRIDE_SPLIT
                for c in range(QKV_WIDTH // LANES):
                    o_ref[0, r, :, c * LANES:(c + 1) * LANES] = (
                        stage2_ref[c, pl.ds(r4 * quarter + hi, sub, stride=STRIDE_SPLIT), :].astype(o_ref.dtype))
            continue
        for r in range(dil):
            for c in range(QKV_WIDTH // LANES):
                o_ref[0, r, :, c * LANES:(c + 1) * LANES] = (
                    stage_ref[c, pl.ds(r, sub, stride=dil), :].astype(o_ref.dtype))


def _project(x, wa, wq, wr, ba, bq, br, cos, sin, batch, seq, tm=1024):
    n, d = x.shape
    tiles = seq // tm
    x_spec = pl.BlockSpec((tm, d), lambda i: (i, 0))
    outs, specs = [], []
    for _, dil in ATTN_GROUPS:
        outs.append(jax.ShapeDtypeStruct((batch, dil, seq // dil, QKV_WIDTH), BF16))
        specs.append(pl.BlockSpec((1, dil, tm // dil, QKV_WIDTH), lambda i: (i // tiles, 0, i % tiles, 0)))
    outs.append(jax.ShapeDtypeStruct((n, wq.shape[1]), BF16))
    specs.append(pl.BlockSpec((tm, wq.shape[1]), lambda i: (i, 0)))
    rot_spec = pl.BlockSpec((tm, RET_QK_DIM // 2), lambda i: (i % tiles, 0))
    qkv = pl.pallas_call(
        _proj_qkv_kernel,
        grid=(n // tm,),
        in_specs=[x_spec, _resident(wa), _resident(wq), _resident(ba), _resident(bq), rot_spec, rot_spec],
        out_specs=specs,
        out_shape=outs,
        scratch_shapes=[pltpu.VMEM((QKV_WIDTH // LANES, tm, LANES), F32),
                        pltpu.VMEM((QKV_WIDTH // LANES, tm, LANES), F32)],
        compiler_params=_params("parallel"),
        name="proj_qkv",
    )(x, wa, wq, ba, bq, cos, sin)
    rest = pl.pallas_call(
        _proj_rest_kernel,
        grid=(n // tm,),
        in_specs=[x_spec, _resident(wr), _resident(br)],
        out_specs=pl.BlockSpec((tm, wr.shape[1]), lambda i: (i, 0)),
        out_shape=jax.ShapeDtypeStruct((n, wr.shape[1]), BF16),
        compiler_params=_params("parallel"),
        name="proj_rest",
    )(x, wr, br)
    return (*qkv, rest)


def _attn_kernel(q_ref, kp_ref, kc_ref, vp_ref, vc_ref, bias_ref, o_ref, lse_ref, *, sub_blocks):
    first_step = pl.program_id(2) == 0
    w = ATTN_BLOCK
    lane = lax.broadcasted_iota(jnp.int32, (w, LANES), 1)
    low_half = lane < HEAD_DIM
    ones = jnp.ones((2 * w, LANES), BF16)
    for r in range(q_ref.shape[1]):
        for j in range(sub_blocks):
            rows = slice(j * w, (j + 1) * w)
            max_tile = jnp.zeros((w, LANES), F32)
            denom_tile = jnp.ones((w, LANES), F32)
            for pair in range(HEADS_PER_GROUP // 2):
                cols = slice(pair * LANES, (pair + 1) * LANES)
                q = q_ref[0, r, rows, cols]
                if j == 0:
                    k = jnp.concatenate([kp_ref[0, r, :, cols], kc_ref[0, r, :w, cols]], axis=0)
                    v = jnp.concatenate([vp_ref[0, r, :, cols], vc_ref[0, r, :w, cols]], axis=0)
                else:
                    k = kc_ref[0, r, (j - 1) * w:(j + 1) * w, cols]
                    v = vc_ref[0, r, (j - 1) * w:(j + 1) * w, cols]
                v_ones = jnp.concatenate([v, ones], axis=1)
                halves = []
                for sel, hh in ((low_half, 0), (~low_half, 1)):
                    h = 2 * pair + hh
                    bias = bias_ref[jnp.where(first_step, 0, 1), h] if j == 0 else bias_ref[1, h]
                    qh = jnp.where(sel, q, jnp.zeros_like(q))
                    s = lax.dot_general(qh, k, (((1,), (1,)), ((), ())), preferred_element_type=F32) + bias
                    m = jnp.max(s, axis=-1, keepdims=True)
                    p = jnp.exp2(s - m).astype(BF16)
                    ov = jnp.dot(p, v_ones, preferred_element_type=F32)
                    halves.append(ov)
                    max_tile = jnp.where(lane == h, m, max_tile)
                    denom_tile = jnp.where(lane == h, ov[:, LANES:], denom_tile)
                numer = jnp.where(low_half, halves[0][:, :LANES], halves[1][:, :LANES])
                denom = jnp.where(low_half, halves[0][:, LANES:], halves[1][:, LANES:])
                o_ref[0, r, rows, cols] = (numer / denom).astype(o_ref.dtype)
            lse_ref[0, r, rows, :] = max_tile * LN2 + jnp.log(denom_tile)


def _attention_group(qkv, bias, step_rows):
    batch, dilation, sub_len, _ = qkv.shape
    q_rows = min(step_rows, sub_len)
    res = min(step_rows // q_rows, dilation)
    sub_blocks = q_rows // ATTN_BLOCK
    steps = sub_len // q_rows

    def cur(unit):
        return pl.BlockSpec((1, res, q_rows, GROUP_WIDTH), lambda b, r, n: (b, r, n, unit))

    def prev(unit):
        return pl.BlockSpec((1, res, ATTN_BLOCK, GROUP_WIDTH),
                            lambda b, r, n: (b, r, jnp.maximum(n * sub_blocks - 1, 0), unit))

    return pl.pallas_call(
        functools.partial(_attn_kernel, sub_blocks=sub_blocks),
        grid=(batch, dilation // res, steps),
        in_specs=[cur(0), prev(1), cur(1), prev(2), cur(2), _full(bias)],
        out_specs=[pl.BlockSpec((1, res, q_rows, GROUP_WIDTH), lambda b, r, n: (b, r, n, 0)),
                   pl.BlockSpec((1, res, q_rows, LANES), lambda b, r, n: (b, r, n, 0))],
        out_shape=[jax.ShapeDtypeStruct((batch, dilation, sub_len, GROUP_WIDTH), BF16),
                   jax.ShapeDtypeStruct((batch, dilation, sub_len, LANES), F32)],
        compiler_params=_params("parallel", "parallel", "arbitrary"),
        name=f"attn_d{dilation}",
    )(qkv, qkv, qkv, qkv, qkv, bias)


def _t5_bucket(dist):
    max_exact = NUM_BUCKETS // 2
    large = max_exact + (np.log(np.maximum(dist, max_exact) / max_exact)
                         / np.log(MAX_DISTANCE / max_exact)
                         * (NUM_BUCKETS - max_exact)).astype(np.int32)
    large = np.minimum(large, NUM_BUCKETS - 1)
    return np.where(dist < max_exact, dist, large).astype(np.int32)


def _attention_bias(rel_bias_group, dilation):
    w = ATTN_BLOCK
    buckets = _t5_bucket((w - np.arange(w + 1)) * dilation)
    onehot = jnp.asarray(np.eye(NUM_BUCKETS, dtype=np.float32)[buckets])
    per_c = jnp.einsum('cb,bh->hc', onehot, rel_bias_group.astype(F32),
                       precision=lax.Precision.HIGHEST) * LOG2E
    period = 2 * w + 1
    vec = jnp.concatenate([per_c, jnp.full((HEADS_PER_GROUP, period - (w + 1)), MASK_VALUE, F32)], axis=1)
    normal = jnp.tile(vec, (1, w))[:, :w * 2 * w].reshape(HEADS_PER_GROUP, w, 2 * w)
    first = jnp.where(np.arange(2 * w)[None, None, :] >= w, normal, MASK_VALUE)
    return jnp.stack([first, normal], axis=0)


def _retention_kernel(q_ref, k_ref, v_ref, g_ref, mask_ref, qdec_ref, kdec_ref, cdec_ref, wr_ref, o_ref,
                      state_ref, lhs_ref, acc_ref):
    @pl.when(pl.program_id(1) == 0)
    def _():
        state_ref[...] = jnp.zeros_like(state_ref)

    rows, chunk = q_ref.shape[0], q_ref.shape[1]
    qk_cols = [slice(h * RET_QK_DIM, (h + 1) * RET_QK_DIM) for h in range(RET_HEADS)]
    v_cols = [slice(h * RET_V_DIM, (h + 1) * RET_V_DIM) for h in range(RET_HEADS)]
    streams = [(b, h) for b in range(rows) for h in range(RET_HEADS)]
    for s, (b, h) in enumerate(streams):
        q = q_ref[b, :, qk_cols[h]]
        k = k_ref[b, :, qk_cols[h]]
        sc = lax.dot_general(q, k, (((1,), (1,)), ((), ())), preferred_element_type=F32) * mask_ref[h]
        lhs_ref[s, :, :chunk] = sc.astype(BF16)
        lhs_ref[s, :, chunk:] = (q.astype(F32) * qdec_ref[h]).astype(BF16)
    for s, (b, h) in enumerate(streams):
        rhs = jnp.concatenate([v_ref[b, :, v_cols[h]], state_ref[s].astype(BF16)], axis=0)
        acc_ref[b, :, v_cols[h]] = jnp.dot(lhs_ref[s], rhs, preferred_element_type=F32)
    for s, (b, h) in enumerate(streams):
        kd = (k_ref[b, :, qk_cols[h]].astype(F32) * kdec_ref[h]).astype(BF16)
        state_ref[s] = state_ref[s] * cdec_ref[h] + lax.dot_general(
            kd, v_ref[b, :, v_cols[h]], (((0,), (0,)), ((), ())), preferred_element_type=F32)
    for h in range(RET_HEADS):
        gated = []
        for b in range(rows):
            o = acc_ref[b, :, v_cols[h]]
            mu = jnp.mean(o, axis=-1, keepdims=True)
            cen = o - mu
            var = jnp.mean(cen * cen, axis=-1, keepdims=True)
            y = cen * lax.rsqrt(var + GN_EPS)
            gated.append(g_ref[b, :, v_cols[h]] * y.astype(BF16))
        part = jnp.dot(jnp.concatenate(gated, axis=0), wr_ref[v_cols[h], :], preferred_element_type=F32)
        proj = part if h == 0 else proj + part
    for b in range(rows):
        o_ref[b] = proj[b * chunk:(b + 1) * chunk].astype(o_ref.dtype)


def _retention_constants(chunk):
    log_g = np.log(1.0 - 2.0 ** (-5.0 - np.arange(RET_HEADS, dtype=np.float64)))
    n = np.arange(chunk, dtype=np.float64)
    diff = n[:, None] - n[None, :]
    mask = np.where(diff >= 0, np.exp(log_g[:, None, None] * np.maximum(diff, 0.0)), 0.0)
    q_dec = np.exp(log_g[:, None] * (n + 1.0))
    k_dec = np.exp(log_g[:, None] * (chunk - 1.0 - n))
    c_dec = np.exp(log_g * chunk)
    return (jnp.asarray(mask, F32),
            jnp.asarray(np.broadcast_to(q_dec[:, :, None], (RET_HEADS, chunk, RET_QK_DIM)), F32),
            jnp.asarray(np.broadcast_to(k_dec[:, :, None], (RET_HEADS, chunk, RET_QK_DIM)), F32),
            jnp.asarray(np.broadcast_to(c_dec[:, None, None], (RET_HEADS, 1, RET_V_DIM)), F32))


def _retention(rqk, rest, wr, batch, seq, chunk):
    consts = _retention_constants(chunk)
    rqk3 = rqk.reshape(batch, seq, 2 * RET_QK_WIDTH)
    rest3 = rest.reshape(batch, seq, rest.shape[1])
    rows = RET_BATCH_ROWS if batch % RET_BATCH_ROWS == 0 else 1
    out = pl.pallas_call(
        _retention_kernel,
        grid=(batch // rows, seq // chunk),
        in_specs=[pl.BlockSpec((rows, chunk, RET_QK_WIDTH), lambda b, c: (b, c, 0)),
                  pl.BlockSpec((rows, chunk, RET_QK_WIDTH), lambda b, c: (b, c, 1)),
                  pl.BlockSpec((rows, chunk, RET_V_WIDTH), lambda b, c: (b, c, 0)),
                  pl.BlockSpec((rows, chunk, RET_V_WIDTH), lambda b, c: (b, c, 1)),
                  *[_full(a) for a in consts], _resident(wr)],
        out_specs=pl.BlockSpec((rows, chunk, D_MODEL), lambda b, c: (b, c, 0)),
        out_shape=jax.ShapeDtypeStruct((batch, seq, D_MODEL), BF16),
        scratch_shapes=[pltpu.VMEM((rows * RET_HEADS, RET_QK_DIM, RET_V_DIM), F32),
                        pltpu.VMEM((rows * RET_HEADS, chunk, chunk + RET_QK_DIM), BF16),
                        pltpu.VMEM((rows, chunk, RET_V_WIDTH), F32)],
        compiler_params=_params("parallel", "arbitrary"),
        name="retention",
    )(rqk3, rqk3, rest3, rest3, *consts, wr)
    return out.reshape(batch * seq, D_MODEL)


def _layer_norm(y, g, b):
    mu = jnp.mean(y, axis=-1, keepdims=True)
    cen = y - mu
    var = jnp.mean(cen * cen, axis=-1, keepdims=True)
    return cen * lax.rsqrt(var + LN_EPS) * g + b


def _to_token_order(dst_ref, src_ref, tmp_ref):
    dil, sub = src_ref.shape[1], src_ref.shape[2]
    for c in range(dst_ref.shape[0]):
        lanes = slice(c * LANES, (c + 1) * LANES)
        if dil == STRIDE_SPLIT * STRIDE_SPLIT:
            quarter = dil * sub // STRIDE_SPLIT
            for r in range(dil):
                r4, hi = r % STRIDE_SPLIT, r // STRIDE_SPLIT
                tmp_ref[c, pl.ds(r4 * quarter + hi, sub, stride=STRIDE_SPLIT), :] = src_ref[0, r, :, lanes].astype(F32)
            for r4 in range(STRIDE_SPLIT):
                dst_ref[c, pl.ds(r4, quarter, stride=STRIDE_SPLIT), :] = tmp_ref[c, r4 * quarter:(r4 + 1) * quarter, :]
        else:
            for r in range(dil):
                dst_ref[c, pl.ds(r, sub, stride=dil), :] = src_ref[0, r, :, lanes].astype(F32)


def _slab_rows(ref, rows):
    return jnp.concatenate([ref[c, rows, :] for c in range(ref.shape[0])], axis=1)


def _merge_ffn_kernel(o0_ref, o1_ref, o2_ref, l0_ref, l1_ref, l2_ref, bb_ref, gates_ref, x_ref,
                      expand_ref, wa_ref, wo_ref, g1_ref, b1_ref, wg_ref, wu_ref, wd_ref, g2_ref, b2_ref, out_ref,
                      o1_tok, o2_tok, l1_tok, l2_tok, tmp_tok):
    for dst, src in ((o1_tok, o1_ref), (o2_tok, o2_ref), (l1_tok, l1_ref), (l2_tok, l2_ref)):
        _to_token_order(dst, src, tmp_tok)
    expand = expand_ref[...]
    sub = out_ref.shape[0] // MERGE_SUBTILES
    mixed = []
    for t in range(MERGE_SUBTILES):
        rows = slice(t * sub, (t + 1) * sub)
        outs = (o0_ref[0, 0, rows, :].astype(F32), _slab_rows(o1_tok, rows), _slab_rows(o2_tok, rows))
        lses = (l0_ref[0, 0, rows, :], l1_tok[0, rows, :], l2_tok[0, rows, :])
        m = jnp.maximum(jnp.maximum(lses[0], lses[1]), lses[2])
        es = [jnp.exp(l - m) for l in lses]
        inv = 1.0 / (es[0] + es[1] + es[2])
        y_a = jnp.zeros(outs[0].shape, F32)
        for e, o in zip(es, outs):
            wgt = e * inv
            hi = wgt.astype(BF16)
            lo = (wgt - hi.astype(F32)).astype(BF16)
            wide = jnp.dot(jnp.concatenate([hi, lo], axis=1), expand, preferred_element_type=F32)
            y_a = y_a + wide * o
        gates = gates_ref[rows, :].astype(F32)
        branch_a = jnp.dot(y_a.astype(BF16), wa_ref[...], preferred_element_type=F32)
        merged = gates[:, :D_MODEL] * branch_a + gates[:, D_MODEL:] * bb_ref[rows, :].astype(F32)
        mix = jnp.dot(merged.astype(BF16), wo_ref[...], preferred_element_type=F32)
        mixed.append(_layer_norm(ALPHA * x_ref[rows, :] + mix, g1_ref[...], b1_ref[...]))
    for t, x1 in enumerate(mixed):
        rows = slice(t * sub, (t + 1) * sub)
        xb = x1.astype(BF16)
        gate = jnp.dot(xb, wg_ref[...], preferred_element_type=F32)
        up = jnp.dot(xb, wu_ref[...], preferred_element_type=F32)
        hidden = (gate * _sigmoid(gate) * up).astype(BF16)
        ffn = jnp.dot(hidden, wd_ref[...], preferred_element_type=F32)
        out_ref[rows, :] = _layer_norm(ALPHA * x1 + ffn, g2_ref[...], b2_ref[...])


def _merge_ffn(os_, lses, branch_b, rest, x, wa, wo, g1, b1, wg, wu, wd, g2, b2, seq, tm=512):
    n = x.shape[0]
    tiles = seq // tm
    expand = jnp.asarray(
        (np.arange(2 * LANES)[:, None] % LANES == (np.arange(GROUP_WIDTH)[None, :] // HEAD_DIM)).astype(np.float32),
        BF16)
    gate_block = (2 * RET_V_WIDTH) // (2 * D_MODEL)
    row = lambda width: pl.BlockSpec((tm, width), lambda i: (i, 0))

    def strided(a):
        dil, width = a.shape[1], a.shape[3]
        return pl.BlockSpec((1, dil, tm // dil, width), lambda i: (i // tiles, 0, i % tiles, 0))

    return pl.pallas_call(
        _merge_ffn_kernel,
        grid=(n // tm,),
        in_specs=[strided(a) for a in os_] + [strided(a) for a in lses]
                 + [row(D_MODEL), pl.BlockSpec((tm, 2 * D_MODEL), lambda i: (i, gate_block)), row(D_MODEL),
                    _full(expand), _resident(wa), _resident(wo), _full(g1), _full(b1),
                    _resident(wg), _resident(wu), _resident(wd), _full(g2), _full(b2)],
        out_specs=row(D_MODEL),
        out_shape=jax.ShapeDtypeStruct((n, D_MODEL), F32),
        scratch_shapes=[pltpu.VMEM((GROUP_WIDTH // LANES, tm, LANES), F32),
                        pltpu.VMEM((GROUP_WIDTH // LANES, tm, LANES), F32),
                        pltpu.VMEM((1, tm, LANES), F32), pltpu.VMEM((1, tm, LANES), F32),
                        pltpu.VMEM((GROUP_WIDTH // LANES, tm, LANES), F32)],
        compiler_params=_params("parallel"),
        name="merge_ffn",
    )(*os_, *lses, branch_b, rest, x, expand, wa, wo, g1, b1, wg, wu, wd, g2, b2)


def _rope_tables(seq):
    half = RET_QK_DIM // 2
    inv_freq = ROPE_BASE ** (-np.arange(half, dtype=np.float64) / half)
    ang = np.arange(seq, dtype=np.float64)[:, None] * inv_freq[None]
    return jnp.asarray(np.cos(ang), F32), jnp.asarray(np.sin(ang), F32)


def _group_major(a):
    parts = [a[..., (t * N_GROUPS + gi) * GROUP_WIDTH:(t * N_GROUPS + gi + 1) * GROUP_WIDTH]
             for gi in range(N_GROUPS) for t in range(3)]
    return jnp.concatenate(parts, axis=-1)


def kernel(x, rel_bias, w_in, b_in, w_attn_proj, w_ret_proj, w_out, ln1_g, ln1_b,
           w_ffn_gate, w_ffn_up, w_ffn_down, ln2_g, ln2_b):
    batch, seq, d = x.shape
    n = batch * seq
    assert d == D_MODEL and seq % 2048 == 0
    cos, sin = _rope_tables(seq)
    biases = [_attention_bias(rel_bias[:, gi * HEADS_PER_GROUP:(gi + 1) * HEADS_PER_GROUP], dil)
              for gi, (_, dil) in enumerate(ATTN_GROUPS)]
    xf = x.reshape(n, d)
    for l in range(DEPTH):
        w = w_in[l]
        bias_in = b_in[l].reshape(1, -1)
        *qkvs, rqk, rest = _project(
            xf, _group_major(w[:, :_ATTN_COLS]).astype(BF16), w[:, _COL_RQK[0]:_COL_RQK[1]].astype(BF16),
            w[:, _COL_REST[0]:_COL_REST[1]].astype(BF16),
            _group_major(bias_in[:, :_ATTN_COLS]), bias_in[:, _COL_RQK[0]:_COL_RQK[1]],
            bias_in[:, _COL_REST[0]:_COL_REST[1]], cos, sin, batch, seq)
        os_, lses = [], []
        for qkv, bias in zip(qkvs, biases):
            o, lse = _attention_group(qkv, bias, step_rows=2048)
            os_.append(o)
            lses.append(lse)
        branch_b = _retention(rqk, rest, w_ret_proj[l].astype(BF16), batch, seq, RET_CHUNK)
        xf = _merge_ffn(os_, lses, branch_b, rest, xf,
                        w_attn_proj[l].astype(BF16), w_out[l].astype(BF16),
                        ln1_g[l].reshape(1, d), ln1_b[l].reshape(1, d),
                        w_ffn_gate[l].astype(BF16), w_ffn_up[l].astype(BF16), w_ffn_down[l].astype(BF16),
                        ln2_g[l].reshape(1, d), ln2_b[l].reshape(1, d), seq)
    return xf.reshape(batch, seq, d)
```
